```python
import jax, jax.numpy as jnp
from jax import lax
import numpy as np

D_MODEL = 1024
BATCH = 4
SEQ = 4096
DEPTH = 2

CHUNK = 64
N_META = 16
D_FF = 4 * D_MODEL
EPS = 1e-5

CONV_DIM = D_MODEL // 2
CONV_WIDTH = 31
POOL_DIM = D_MODEL // 2
POOL_WINDOWS = (2, 4, 8, 16)
POOL_GROUP = POOL_DIM // len(POOL_WINDOWS)
EVEN_IN = 2 * CONV_DIM + POOL_DIM
EVEN_MIX = CONV_DIM + POOL_DIM

GLA_HEADS = 4
GLA_DK = D_MODEL // 2
GLA_DV = D_MODEL
GLA_HK = GLA_DK // GLA_HEADS
GLA_HV = GLA_DV // GLA_HEADS
GLA_GATE_RANK = 16
GLA_GATE_NORM = 16.0
ODD_IN = 2 * GLA_DK + 2 * GLA_DV + GLA_GATE_RANK

N_EVEN = (DEPTH + 1) // 2
N_ODD = DEPTH // 2

kernel_name = "hybrid_conv_pool_gla_trunk"


def rms_norm(x, g):
    xf = x.astype(jnp.float32)
    y = xf * lax.rsqrt(jnp.mean(xf * xf, axis=-1, keepdims=True) + EPS)
    return (y * g.astype(jnp.float32)).astype(x.dtype)


def layer_norm(x, g, b):
    xf = x.astype(jnp.float32)
    mu = jnp.mean(xf, axis=-1, keepdims=True)
    xc = xf - mu
    y = xc * lax.rsqrt(jnp.mean(xc * xc, axis=-1, keepdims=True) + EPS)
    return (y * g.astype(jnp.float32) + b.astype(jnp.float32)).astype(x.dtype)


def causal_depthwise_conv(u, w, b):
    c = u.shape[-1]
    lhs = jnp.pad(u, ((0, 0), (CONV_WIDTH - 1, 0), (0, 0)))
    out = lax.conv_general_dilated(
        lhs, w[:, None, :].astype(u.dtype), window_strides=(1,), padding="VALID",
        dimension_numbers=("NWC", "WIO", "NWC"), feature_group_count=c)
    return out + b.astype(u.dtype)


def trailing_mean_minus_self(u, window):
    L = u.shape[1]
    uf = u.astype(jnp.float32)
    csum = jnp.cumsum(uf, axis=1)
    cpad = jnp.pad(csum, ((0, 0), (1, 0), (0, 0)))
    lower = jnp.pad(cpad[:, :L - window + 1], ((0, 0), (window - 1, 0), (0, 0)))
    count = jnp.minimum(jnp.arange(L) + 1, window).astype(jnp.float32)[None, :, None]
    return ((csum - lower) / count - uf).astype(u.dtype)


def conv_pool_mixer(h, w_in, conv_w, conv_b, ln_g, ln_b, pool_w, pool_scale, w_out):
    bsz, L, _ = h.shape
    z = h @ w_in
    a_in, p_in = z[..., :2 * CONV_DIM], z[..., 2 * CONV_DIM:]
    a = a_in[..., :CONV_DIM] * jax.nn.sigmoid(a_in[..., CONV_DIM:])
    a = causal_depthwise_conv(a, conv_w, conv_b)
    a = jax.nn.silu(layer_norm(a, ln_g, ln_b))
    groups = [trailing_mean_minus_self(p_in[..., i * POOL_GROUP:(i + 1) * POOL_GROUP], w)
              for i, w in enumerate(POOL_WINDOWS)]
    p = jnp.stack(groups, axis=2)
    p = jnp.einsum("blgc,gcd->blgd", p, pool_w).reshape(bsz, L, POOL_DIM) * pool_scale
    return jnp.concatenate([a, p], axis=-1) @ w_out


def gla_mixer(h, w_in, gate_w2, gate_b, head_g, w_out):
    bsz, L, _ = h.shape
    z = h @ w_in
    split_at = np.cumsum([GLA_DK, GLA_DK, GLA_DV, GLA_DV]).tolist()
    q, k, v, g, r = jnp.split(z, split_at, axis=-1)
    log_a = jax.nn.log_sigmoid((r @ gate_w2 + gate_b).astype(jnp.float32)) / GLA_GATE_NORM
    pad = (-L) % CHUNK
    n_chunks = (L + pad) // CHUNK

    def to_chunks(t, hd):
        t = jnp.pad(t.astype(jnp.float32), ((0, 0), (pad, 0), (0, 0)))
        return t.reshape(bsz, n_chunks, CHUNK, GLA_HEADS, hd).transpose(1, 0, 3, 2, 4)

    qc = to_chunks(q * (GLA_HK ** -0.5), GLA_HK)
    kc = to_chunks(k, GLA_HK)
    vc = to_chunks(v, GLA_HV)
    lac = to_chunks(log_a, GLA_HK)
    cum = jnp.cumsum(lac, axis=3)
    total = cum[:, :, :, -1]
    k_dec = kc * jnp.exp(total[:, :, :, None, :] - cum)

    def step(state, inp):
        q_i, k_i, v_i, tot_i = inp
        state = jnp.exp(tot_i)[..., None] * state + jnp.einsum("bhck,bhcv->bhkv", k_i, v_i)
        o_i = jnp.einsum("bhck,bhkv->bhcv", q_i, state)
        return state, o_i

    s0 = jnp.zeros((bsz, GLA_HEADS, GLA_HK, GLA_HV), jnp.float32)
    _, o = lax.scan(step, s0, (qc, k_dec, vc, total))
    o = o.transpose(1, 0, 3, 2, 4).reshape(bsz, n_chunks * CHUNK, GLA_HEADS, GLA_HV)[:, pad:]
    o = rms_norm(o, head_g).reshape(bsz, L, GLA_DV)
    o = o * jax.nn.silu(g.astype(jnp.float32))
    return o.astype(h.dtype) @ w_out


def squared_relu_mlp(u, w1, w2):
    a = jax.nn.relu(u @ w1)
    return (a * a) @ w2


def setup_inputs(seed: int = 0) -> dict:
    key = jax.random.key(seed)
    ks = jax.random.split(key, 21)
    f32 = jnp.float32
    nrm = lambda k, shape, scale: jax.random.normal(k, shape, f32) * scale
    return {
        "x": nrm(ks[0], (BATCH, SEQ, D_MODEL), 1.0),
        "meta_tokens": nrm(ks[1], (N_META, D_MODEL), 1.0),
        "mix_norm_g": 1.0 + nrm(ks[2], (DEPTH, D_MODEL), 0.02),
        "ffn_norm_g": 1.0 + nrm(ks[3], (DEPTH, D_MODEL), 0.02),
        "ffn_w1": nrm(ks[4], (DEPTH, D_MODEL, D_FF), D_MODEL ** -0.5),
        "ffn_w2": nrm(ks[5], (DEPTH, D_FF, D_MODEL), D_FF ** -0.5),
        "cp_w_in": nrm(ks[6], (N_EVEN, D_MODEL, EVEN_IN), D_MODEL ** -0.5),
        "cp_conv_w": nrm(ks[7], (N_EVEN, CONV_WIDTH, CONV_DIM), CONV_WIDTH ** -0.5),
        "cp_conv_b": nrm(ks[8], (N_EVEN, CONV_DIM), 0.02),
        "cp_ln_g": 1.0 + nrm(ks[9], (N_EVEN, CONV_DIM), 0.02),
        "cp_ln_b": nrm(ks[10], (N_EVEN, CONV_DIM), 0.02),
        "cp_pool_w": nrm(ks[11], (N_EVEN, len(POOL_WINDOWS), POOL_GROUP, POOL_GROUP), POOL_GROUP ** -0.5),
        "cp_pool_scale": 1.0 + nrm(ks[12], (N_EVEN, POOL_DIM), 0.02),
        "cp_w_out": nrm(ks[13], (N_EVEN, EVEN_MIX, D_MODEL), EVEN_MIX ** -0.5),
        "gla_w_in": nrm(ks[14], (N_ODD, D_MODEL, ODD_IN), D_MODEL ** -0.5),
        "gla_gate_w2": nrm(ks[15], (N_ODD, GLA_GATE_RANK, GLA_DK), GLA_GATE_RANK ** -0.5),
        "gla_gate_b": nrm(ks[16], (N_ODD, GLA_DK), 0.02),
        "gla_head_g": 1.0 + nrm(ks[17], (N_ODD, GLA_HV), 0.02),
        "gla_w_out": nrm(ks[18], (N_ODD, GLA_DV, D_MODEL), GLA_DV ** -0.5),
        "final_norm_g": 1.0 + nrm(ks[19], (D_MODEL,), 0.02),
    }


def reference(x, meta_tokens, mix_norm_g, ffn_norm_g, ffn_w1, ffn_w2, cp_w_in, cp_conv_w,
              cp_conv_b, cp_ln_g, cp_ln_b, cp_pool_w, cp_pool_scale, cp_w_out, gla_w_in,
              gla_gate_w2, gla_gate_b, gla_head_g, gla_w_out, final_norm_g):
    bsz = x.shape[0]
    meta = jnp.broadcast_to(meta_tokens[None].astype(x.dtype), (bsz, N_META, D_MODEL))
    h = jnp.concatenate([meta, x], axis=1)
    for i in range(DEPTH):
        j = i // 2
        u = rms_norm(h, mix_norm_g[i])
        if i % 2 == 0:
            h = h + conv_pool_mixer(u, cp_w_in[j], cp_conv_w[j], cp_conv_b[j], cp_ln_g[j],
                                    cp_ln_b[j], cp_pool_w[j], cp_pool_scale[j], cp_w_out[j])
        else:
            h = h + gla_mixer(u, gla_w_in[j], gla_gate_w2[j], gla_gate_b[j], gla_head_g[j],
                              gla_w_out[j])
        u = rms_norm(h, ffn_norm_g[i])
        h = h + squared_relu_mlp(u, ffn_w1[i], ffn_w2[i])
    return rms_norm(h[:, N_META:], final_norm_g)
```

```python
import functools

import jax
import jax.numpy as jnp
from jax import lax
from jax.experimental import pallas as pl
from jax.experimental.pallas import tpu as pltpu

D_MODEL = 1024
N_META = 16
CHUNK = 64
D_FF = 4 * D_MODEL
EPS = 1e-5

CONV_DIM = D_MODEL // 2
CONV_WIDTH = 31
POOL_DIM = D_MODEL // 2
POOL_WINDOWS = (2, 4, 8, 16)
POOL_GROUP = POOL_DIM // len(POOL_WINDOWS)

GLA_HEADS = 4
GLA_DK = D_MODEL // 2
GLA_DV = D_MODEL
GLA_HK = GLA_DK // GLA_HEADS
GLA_HV = GLA_DV // GLA_HEADS
GLA_GATE_RANK = 16
GLA_GATE_NORM = 16.0

LANES = 128
A_CARRY = 32
P_CARRY = 16
ROW_BLOCK = 64
VMEM_LIMIT = 56 * 1024 * 1024

F32 = jnp.float32
BF16 = jnp.bfloat16


def _rms(x, g):
    return x * lax.rsqrt(jnp.mean(x * x, axis=-1, keepdims=True) + EPS) * g


def _sigmoid(x):
    return 1.0 / (1.0 + jnp.exp(-x))


def _const_spec(shape):
    zeros = (0,) * len(shape)
    return pl.BlockSpec(shape, lambda *_: zeros)


def _l0_kernel(h_ref, g_ref, win_ref, cw_ref, cb_ref, lng_ref, lnb_ref, pw_ref, ps_ref, wout_ref,
               a0_ref, p0_ref, out_ref, atail_ref, ptail_ref,
               abuf, pbuf, mix, *, tile, l_off):
    t = pl.program_id(1)

    @pl.when(t == 0)
    def _():
        abuf[0:A_CARRY, :] = a0_ref[...]
        pbuf[0:P_CARRY, :] = p0_ref[...]

    u = _rms(h_ref[0], g_ref[...]).astype(BF16)
    z = jnp.dot(u, win_ref[...], preferred_element_type=F32)
    abuf[A_CARRY:A_CARRY + tile, :] = z[:, :CONV_DIM] * _sigmoid(z[:, CONV_DIM:2 * CONV_DIM])
    pbuf[P_CARRY:P_CARRY + tile, :] = z[:, 2 * CONV_DIM:]

    rb = min(ROW_BLOCK, tile)

    def block(i, carry):
        r0 = pl.multiple_of(i * rb, rb)
        acc = jnp.broadcast_to(cb_ref[...], (rb, CONV_DIM))
        for b in range(8):
            n = rb if b == 0 else rb + 8
            part = None
            for a in range(A_CARRY // 8 + 1):
                j = 8 * a + b - (A_CARRY - (CONV_WIDTH - 1))
                if 0 <= j < CONV_WIDTH:
                    term = cw_ref[j:j + 1, :] * abuf[pl.ds(r0 + 8 * a, n), :]
                    part = term if part is None else part + term
            acc = acc + (part if b == 0 else pltpu.roll(part, n - b, axis=0)[0:rb])
        mu = jnp.mean(acc, axis=-1, keepdims=True)
        xc = acc - mu
        y = xc * lax.rsqrt(jnp.mean(xc * xc, axis=-1, keepdims=True) + EPS)
        y = y * lng_ref[...] + lnb_ref[...]
        mix[pl.ds(r0, rb), 0:CONV_DIM] = (y * _sigmoid(y)).astype(BF16)
        for gi, w in enumerate(POOL_WINDOWS):
            lo, hi = gi * POOL_GROUP, (gi + 1) * POOL_GROUP
            win = pbuf[pl.ds(r0, P_CARRY + rb), lo:hi]
            cur = win[P_CARRY:]
            ws = win
            for step in range(w.bit_length() - 1):
                ws = ws + pltpu.roll(ws, 1 << step, axis=0)
            ws = ws[P_CARRY:]
            if l_off + 1 >= w:
                mean = ws * (1.0 / w)
            else:
                pos = l_off + t * tile + r0 + lax.broadcasted_iota(jnp.int32, (rb, POOL_GROUP), 0)
                mean = ws / jnp.minimum(pos + 1, w).astype(F32)
            mix[pl.ds(r0, rb), CONV_DIM + lo:CONV_DIM + hi] = (mean - cur).astype(BF16)
        return carry

    lax.fori_loop(0, tile // rb, block, 0)

    pooled = jnp.dot(mix[:, CONV_DIM:], pw_ref[...], preferred_element_type=F32) * ps_ref[...]
    y = jnp.dot(mix[:, 0:CONV_DIM], wout_ref[0:CONV_DIM, :], preferred_element_type=F32)
    y = y + jnp.dot(pooled.astype(BF16), wout_ref[CONV_DIM:, :], preferred_element_type=F32)
    out_ref[0] = h_ref[0] + y

    a_tail = abuf[tile:tile + A_CARRY, :]
    p_tail = pbuf[tile:tile + P_CARRY, :]
    abuf[0:A_CARRY, :] = a_tail
    pbuf[0:P_CARRY, :] = p_tail
    atail_ref[...] = a_tail
    ptail_ref[...] = p_tail


def _l0_mixer(h, g, win, cw, cb, lng, lnb, pw_bd, ps, wout, a0, p0, *, tile, l_off):
    b, s, d = h.shape
    assert s % tile == 0 and tile % min(ROW_BLOCK, tile) == 0
    kern = functools.partial(_l0_kernel, tile=tile, l_off=l_off)
    row_spec = pl.BlockSpec((1, tile, d), lambda i, j: (i, j, 0))
    return pl.pallas_call(
        kern,
        grid=(b, s // tile),
        in_specs=[row_spec, _const_spec(g.shape), _const_spec(win.shape), _const_spec(cw.shape),
                  _const_spec(cb.shape), _const_spec(lng.shape), _const_spec(lnb.shape),
                  _const_spec(pw_bd.shape), _const_spec(ps.shape), _const_spec(wout.shape),
                  _const_spec(a0.shape), _const_spec(p0.shape)],
        out_specs=[row_spec, _const_spec((A_CARRY, CONV_DIM)), _const_spec((P_CARRY, POOL_DIM))],
        out_shape=[jax.ShapeDtypeStruct(h.shape, F32),
                   jax.ShapeDtypeStruct((A_CARRY, CONV_DIM), F32),
                   jax.ShapeDtypeStruct((P_CARRY, POOL_DIM), F32)],
        scratch_shapes=[pltpu.VMEM((A_CARRY + tile, CONV_DIM), F32),
                        pltpu.VMEM((P_CARRY + tile, POOL_DIM), F32),
                        pltpu.VMEM((tile, CONV_DIM + POOL_DIM), BF16)],
        compiler_params=pltpu.CompilerParams(dimension_semantics=("arbitrary", "arbitrary"),
                                             vmem_limit_bytes=VMEM_LIMIT),
        name="l0_mixer",
    )(h, g, win, cw, cb, lng, lnb, pw_bd, ps, wout, a0, p0)


def _mlp_kernel(h_ref, g_ref, w1_ref, w2_ref, fg_ref, out_ref, *, ff_chunk, final_norm):
    h = h_ref[...]
    u = _rms(h, g_ref[...]).astype(BF16)
    acc = h
    for c in range(D_FF // ff_chunk):
        lo = c * ff_chunk
        a = jnp.maximum(jnp.dot(u, w1_ref[:, lo:lo + ff_chunk], preferred_element_type=F32), 0.0)
        acc = acc + jnp.dot((a * a).astype(BF16), w2_ref[lo:lo + ff_chunk, :],
                            preferred_element_type=F32)
    if final_norm:
        acc = _rms(acc, fg_ref[...])
    out_ref[...] = acc


def _mlp(h2d, g, w1, w2, fg, *, tile, final_norm, ff_chunk=512):
    n, d = h2d.shape
    assert n % tile == 0
    kern = functools.partial(_mlp_kernel, ff_chunk=ff_chunk, final_norm=final_norm)
    row_spec = pl.BlockSpec((tile, d), lambda i: (i, 0))
    return pl.pallas_call(
        kern,
        grid=(n // tile,),
        in_specs=[row_spec, _const_spec(g.shape), _const_spec(w1.shape), _const_spec(w2.shape),
                  _const_spec(fg.shape)],
        out_specs=row_spec,
        out_shape=jax.ShapeDtypeStruct(h2d.shape, F32),
        compiler_params=pltpu.CompilerParams(dimension_semantics=("arbitrary",),
                                             vmem_limit_bytes=VMEM_LIMIT),
        name="mlp",
    )(h2d, g, w1, w2, fg)


def _l1_kernel(h_ref, g_ref, wqkvg_ref, wr_ref, gw2_ref, gb_ref, hg_ref, wout_ref, s0_ref, tri_ref,
               out_ref, sfin_ref, z_scr, la_scr, o_scr, state, *, tile, n_pad):
    t = pl.program_id(1)

    @pl.when(t == 0)
    def _():
        state[...] = s0_ref[...]

    u = _rms(h_ref[0], g_ref[...]).astype(BF16)
    z_scr[...] = jnp.dot(u, wqkvg_ref[...], preferred_element_type=F32)
    r = jnp.dot(u, wr_ref[...], preferred_element_type=F32)
    x = jnp.dot(r.astype(BF16), gw2_ref[...], preferred_element_type=F32) + gb_ref[...]
    la = (jnp.minimum(x, 0.0) - jnp.log1p(jnp.exp(-jnp.abs(x)))) * (1.0 / GLA_GATE_NORM)
    if n_pad:
        row = t * tile + lax.broadcasted_iota(jnp.int32, (tile, GLA_DK), 0)
        la = jnp.where(row >= n_pad, la, 0.0)
    la_scr[...] = la

    def chunk(c, carry):
        r0 = pl.multiple_of(c * CHUNK, CHUNK)
        rows = pl.ds(r0, CHUNK)
        cum = jnp.dot(tri_ref[...], la_scr[rows, :], preferred_element_type=F32,
                      precision=lax.Precision.HIGHEST)
        tot = cum[CHUNK - 1:CHUNK, :]
        kd = (z_scr[rows, GLA_DK:2 * GLA_DK] * jnp.exp(tot - cum)).astype(BF16)
        q = (z_scr[rows, 0:GLA_DK] * (GLA_HK ** -0.5)).astype(BF16)
        etot = jnp.exp(tot)
        for hh in range(GLA_HEADS):
            klo, vlo = hh * GLA_HK, 2 * GLA_DK + hh * GLA_HV
            v_h = z_scr[rows, vlo:vlo + GLA_HV].astype(BF16)
            upd = lax.dot_general(v_h, kd[:, klo:klo + GLA_HK], (((0,), (0,)), ((), ())),
                                  preferred_element_type=F32)
            s_h = state[hh] * etot[:, klo:klo + GLA_HK] + upd
            state[hh] = s_h
            o_scr[rows, hh * GLA_HV:(hh + 1) * GLA_HV] = lax.dot_general(
                q[:, klo:klo + GLA_HK], s_h.astype(BF16), (((1,), (1,)), ((), ())),
                preferred_element_type=F32)
        return carry

    lax.fori_loop(0, tile // CHUNK, chunk, 0)

    glo = 2 * GLA_DK + GLA_DV
    rb = min(ROW_BLOCK, tile)

    def gate(i, carry):
        rows = pl.ds(pl.multiple_of(i * rb, rb), rb)
        for hh in range(GLA_HEADS):
            cols = slice(hh * GLA_HV, (hh + 1) * GLA_HV)
            o = o_scr[rows, cols]
            o = o * lax.rsqrt(jnp.mean(o * o, axis=-1, keepdims=True) + EPS) * hg_ref[...]
            gt = z_scr[rows, glo + hh * GLA_HV:glo + (hh + 1) * GLA_HV]
            o_scr[rows, cols] = o * (gt * _sigmoid(gt))
        return carry

    lax.fori_loop(0, tile // rb, gate, 0)

    out_ref[0] = h_ref[0] + jnp.dot(o_scr[...].astype(BF16), wout_ref[...],
                                    preferred_element_type=F32)
    sfin_ref[...] = state[...]


def _l1_mixer(h, g, wqkvg, wr, gw2, gb, hg, wout, s0, tri, *, tile, n_pad):
    b, s, d = h.shape
    assert s % tile == 0 and tile % CHUNK == 0
    kern = functools.partial(_l1_kernel, tile=tile, n_pad=n_pad)
    row_spec = pl.BlockSpec((1, tile, d), lambda i, j: (i, j, 0))
    state_shape = (GLA_HEADS, GLA_HV, GLA_HK)
    return pl.pallas_call(
        kern,
        grid=(b, s // tile),
        in_specs=[row_spec, _const_spec(g.shape), _const_spec(wqkvg.shape), _const_spec(wr.shape),
                  _const_spec(gw2.shape), _const_spec(gb.shape), _const_spec(hg.shape),
                  _const_spec(wout.shape), _const_spec(s0.shape), _const_spec(tri.shape)],
        out_specs=[row_spec, _const_spec(state_shape)],
        out_shape=[jax.ShapeDtypeStruct(h.shape, F32), jax.ShapeDtypeStruct(state_shape, F32)],
        scratch_shapes=[pltpu.VMEM((tile, 2 * GLA_DK + 2 * GLA_DV), F32),
                        pltpu.VMEM((tile, GLA_DK), F32),
                        pltpu.VMEM((tile, GLA_DV), F32),
                        pltpu.VMEM(state_shape, F32)],
        compiler_params=pltpu.CompilerParams(dimension_semantics=("arbitrary", "arbitrary"),
                                             vmem_limit_bytes=VMEM_LIMIT),
        name="l1_mixer",
    )(h, g, wqkvg, wr, gw2, gb, hg, wout, s0, tri)


def kernel(x, meta_tokens, mix_norm_g, ffn_norm_g, ffn_w1, ffn_w2, cp_w_in, cp_conv_w, cp_conv_b,
           cp_ln_g, cp_ln_b, cp_pool_w, cp_pool_scale, cp_w_out, gla_w_in, gla_gate_w2, gla_gate_b,
           gla_head_g, gla_w_out, final_norm_g):
    bsz, seq, d = x.shape
    row = lambda v: v.reshape(1, -1).astype(F32)

    w1 = ffn_w1.astype(BF16)
    w2 = ffn_w2.astype(BF16)
    cp_win = cp_w_in[0].astype(BF16)
    cp_wout = cp_w_out[0].astype(BF16)
    pw_bd = jnp.zeros((POOL_DIM, POOL_DIM), F32)
    for gi in range(len(POOL_WINDOWS)):
        lo = gi * POOL_GROUP
        pw_bd = lax.dynamic_update_slice(pw_bd, cp_pool_w[0, gi], (lo, lo))
    pw_bd = pw_bd.astype(BF16)
    n_qkvg = 2 * GLA_DK + 2 * GLA_DV
    wqkvg = gla_w_in[0, :, :n_qkvg].astype(BF16)
    wr = jnp.pad(gla_w_in[0, :, n_qkvg:], ((0, 0), (0, LANES - GLA_GATE_RANK))).astype(BF16)
    gw2 = jnp.pad(gla_gate_w2[0], ((0, LANES - GLA_GATE_RANK), (0, 0))).astype(BF16)
    gla_wout = gla_w_out[0].astype(BF16)
    head_g = row(gla_head_g[0])
    tri = jnp.tril(jnp.ones((CHUNK, CHUNK), F32))
    fg = row(final_norm_g)

    l0_args = (row(mix_norm_g[0]), cp_win, cp_conv_w[0].astype(F32), row(cp_conv_b[0]),
               row(cp_ln_g[0]), row(cp_ln_b[0]), pw_bd, row(cp_pool_scale[0]), cp_wout)
    l1_args = (row(mix_norm_g[1]), wqkvg, wr, gw2, row(gla_gate_b[0]), head_g, gla_wout)

    hm = meta_tokens.astype(F32)[None]
    hm, a_tail, p_tail = _l0_mixer(hm, *l0_args, jnp.zeros((A_CARRY, CONV_DIM), F32),
                                   jnp.zeros((P_CARRY, POOL_DIM), F32), tile=N_META, l_off=0)
    hm = _mlp(hm[0], row(ffn_norm_g[0]), w1[0], w2[0], fg, tile=N_META, final_norm=False)
    hm = jnp.pad(hm, ((CHUNK - N_META, 0), (0, 0)))[None]
    s_zero = jnp.zeros((GLA_HEADS, GLA_HV, GLA_HK), F32)
    _, s_meta = _l1_mixer(hm, *l1_args, s_zero, tri, tile=CHUNK, n_pad=CHUNK - N_META)

    tile = 512
    h, _, _ = _l0_mixer(x.astype(F32), *l0_args, a_tail, p_tail, tile=tile, l_off=N_META)
    h = _mlp(h.reshape(bsz * seq, d), row(ffn_norm_g[0]), w1[0], w2[0], fg, tile=tile,
             final_norm=False).reshape(bsz, seq, d)
    h, _ = _l1_mixer(h, *l1_args, s_meta, tri, tile=tile, n_pad=0)
    out = _mlp(h.reshape(bsz * seq, d), row(ffn_norm_g[1]), w1[1], w2[1], fg, tile=tile,
               final_norm=True)
    return out.reshape(bsz, seq, d).astype(x.dtype)
```

```python
import functools

import jax
import jax.numpy as jnp
from jax import lax
from jax.experimental import pallas as pl
from jax.experimental.pallas import tpu as pltpu

D_MODEL = 1024
N_META = 16
CHUNK = 64
D_FF = 4 * D_MODEL
EPS = 1e-5

CONV_DIM = D_MODEL // 2
CONV_WIDTH = 31
POOL_DIM = D_MODEL // 2
POOL_WINDOWS = (2, 4, 8, 16)
POOL_GROUP = POOL_DIM // len(POOL_WINDOWS)

GLA_HEADS = 4
GLA_DK = D_MODEL // 2
GLA_DV = D_MODEL
GLA_HK = GLA_DK // GLA_HEADS
GLA_HV = GLA_DV // GLA_HEADS
GLA_GATE_RANK = 16
GLA_GATE_NORM = 16.0
N_QKVG = 2 * GLA_DK + 2 * GLA_DV

LANES = 128
N_SLAB = CONV_DIM // LANES
assert POOL_GROUP == LANES and POOL_DIM == CONV_DIM
A_CARRY = 32
P_CARRY = 16
ROW_BLOCK = 64
FF_CHUNK = 512
N_FF = D_FF // FF_CHUNK
MLP_ROW_GROUPS = 2
VMEM_LIMIT = 56 * 1024 * 1024

F32 = jnp.float32
BF16 = jnp.bfloat16


def _rms(x, g):
    return x * lax.rsqrt(jnp.mean(x * x, axis=-1, keepdims=True) + EPS) * g


def _sigmoid(x):
    return 1.0 / (1.0 + jnp.exp(-x))


def _const_spec(shape):
    zeros = (0,) * len(shape)
    return pl.BlockSpec(shape, lambda *_: zeros)


def _dependent_zero(v):
    bits = lax.bitcast_convert_type(v, jnp.uint32)
    bits = lax.shift_right_logical(lax.shift_right_logical(bits, jnp.uint32(16)), jnp.uint32(16))
    return lax.bitcast_convert_type(bits, F32)


def _delayed(token, hops=3):
    for _ in range(hops):
        s = jnp.sum(token, axis=-1, keepdims=True)
        token = _dependent_zero(jnp.broadcast_to(s, token.shape))
    return token


def _row_start(i, rows):
    return i * rows if isinstance(i, int) else pl.multiple_of(i * rows, rows)


def _mlp_chunk(c, ubuf, w1c_ref, w2c_ref, out_ref):
    n = ubuf.shape[0]
    groups = MLP_ROW_GROUPS if n % (16 * MLP_ROW_GROUPS) == 0 else 1
    for k in range(groups):
        rows = slice(k * n // groups, (k + 1) * n // groups)
        a = jnp.maximum(jnp.dot(ubuf[rows, :], w1c_ref[c], preferred_element_type=F32), 0.0)
        out_ref[rows, :] += jnp.dot((a * a).astype(BF16), w2c_ref[c], preferred_element_type=F32)


def _paired_loop(n_other, mlp_chunk, other_block):
    if n_other % N_FF == 0:
        per = n_other // N_FF

        def body(c, carry):
            for k in range(per):
                other_block(c * per + k)
            mlp_chunk(c)
            return carry
    else:
        for k in range(n_other):
            other_block(k)

        def body(c, carry):
            mlp_chunk(c)
            return carry

    lax.fori_loop(0, N_FF, body, 0)


def _conv_pool_block(r0, rb, pos0, abuf, pbuf, mix, cw_ref, cb_ref, lng_ref, lnb_ref, l_off):
    conv = []
    for k in range(N_SLAB):
        cols = slice(k * LANES, (k + 1) * LANES)
        acc = jnp.broadcast_to(cb_ref[:, cols], (rb, LANES))
        for b in range(8):
            n = rb if b == 0 else rb + 8
            part = None
            for a in range(A_CARRY // 8 + 1):
                j = 8 * a + b - (A_CARRY - (CONV_WIDTH - 1))
                if 0 <= j < CONV_WIDTH:
                    term = cw_ref[j:j + 1, cols] * abuf[k, pl.ds(r0 + 8 * a, n), :]
                    part = term if part is None else part + term
            acc = acc + (part if b == 0 else pltpu.roll(part, n - b, axis=0)[0:rb])
        conv.append(acc)
    mu = sum(jnp.sum(c, axis=-1, keepdims=True) for c in conv) * (1.0 / CONV_DIM)
    cen = [c - mu for c in conv]
    var = sum(jnp.sum(c * c, axis=-1, keepdims=True) for c in cen) * (1.0 / CONV_DIM)
    rstd = lax.rsqrt(var + EPS)
    for k in range(N_SLAB):
        cols = slice(k * LANES, (k + 1) * LANES)
        y = cen[k] * rstd * lng_ref[:, cols] + lnb_ref[:, cols]
        mix[pl.ds(r0, rb), cols] = (y * _sigmoid(y)).astype(BF16)
    for k, w in enumerate(POOL_WINDOWS):
        win = pbuf[k, pl.ds(r0, P_CARRY + rb), :]
        cur = win[P_CARRY:]
        ws = win
        for step in range(w.bit_length() - 1):
            ws = ws + pltpu.roll(ws, 1 << step, axis=0)
        ws = ws[P_CARRY:]
        if l_off + 1 >= w:
            mean = ws * (1.0 / w)
        else:
            pos = pos0 + r0 + lax.broadcasted_iota(jnp.int32, (rb, LANES), 0)
            mean = ws / jnp.minimum(pos + 1, w).astype(F32)
        mix[pl.ds(r0, rb), CONV_DIM + k * LANES:CONV_DIM + (k + 1) * LANES] = (mean - cur).astype(BF16)


def _slabs(x):
    return jnp.stack([x[:, k * LANES:(k + 1) * LANES] for k in range(N_SLAB)])


def _unslab(x):
    return jnp.concatenate([x[k] for k in range(N_SLAB)], axis=-1)


def _layer0_kernel(x_ref, gm_ref, win_ref, cw_ref, cb_ref, lng_ref, lnb_ref, pw_ref, ps_ref, wout_ref,
                   a0_ref, p0_ref, gf_ref, w1c_ref, w2c_ref, out_ref, atail_ref, ptail_ref,
                   abuf, pbuf, mix, hbuf, ubuf, *, tile, l_off, n_tiles, tiles_per_seq):
    s = pl.program_id(0)
    t_in_seq = lax.rem(s, tiles_per_seq)

    @pl.when(s == 0)
    def _():
        mix[...] = jnp.zeros(mix.shape, mix.dtype)
        hbuf[...] = jnp.zeros(hbuf.shape, hbuf.dtype)

    @pl.when(t_in_seq == 0)
    def _():
        abuf[:, 0:A_CARRY, :] = _slabs(a0_ref[...])
        pbuf[:, 0:P_CARRY, :] = _slabs(p0_ref[...])

    u = _rms(x_ref[...], gm_ref[...]).astype(BF16)
    z = jnp.dot(u, win_ref[...], preferred_element_type=F32)
    glu = z[:, :CONV_DIM] * _sigmoid(z[:, CONV_DIM:2 * CONV_DIM])
    pin = z[:, 2 * CONV_DIM:]

    pooled = jnp.dot(mix[:, CONV_DIM:], pw_ref[...], preferred_element_type=F32) * ps_ref[...]
    y = jnp.dot(mix[:, 0:CONV_DIM], wout_ref[0:CONV_DIM, :], preferred_element_type=F32)
    y = y + jnp.dot(pooled.astype(BF16), wout_ref[CONV_DIM:, :], preferred_element_type=F32)
    h1 = hbuf[...] + y
    uu = _rms(h1, gf_ref[...]).astype(BF16)
    acc = h1
    rb = min(ROW_BLOCK, tile)
    if tile // rb > 1:
        token = _dependent_zero(h1[0:8, 0:LANES])
        for i in range(tile // rb):
            rows = slice(i * rb, (i + 1) * rb)
            for k in range(N_SLAB):
                zero = jnp.tile(token, (rb // 8, 1))
                cols = slice(k * LANES, (k + 1) * LANES)
                abuf[k, A_CARRY + i * rb:A_CARRY + (i + 1) * rb, :] = glu[rows, cols] + zero
                pbuf[k, P_CARRY + i * rb:P_CARRY + (i + 1) * rb, :] = pin[rows, cols] + zero
                token = _delayed(token)
    else:
        abuf[:, A_CARRY:A_CARRY + tile, :] = _slabs(glu)
        pbuf[:, P_CARRY:P_CARRY + tile, :] = _slabs(pin)
    for c in range(N_FF):
        a = jnp.maximum(jnp.dot(uu, w1c_ref[c], preferred_element_type=F32), 0.0)
        acc = acc + jnp.dot((a * a).astype(BF16), w2c_ref[c], preferred_element_type=F32)
    out_ref[...] = acc

    pos0 = l_off + t_in_seq * tile
    for i in range(tile // rb):
        _conv_pool_block(i * rb, rb, pos0, abuf, pbuf, mix, cw_ref, cb_ref, lng_ref, lnb_ref, l_off)

    hbuf[...] = x_ref[...]
    a_tail = abuf[:, tile:tile + A_CARRY, :]
    p_tail = pbuf[:, tile:tile + P_CARRY, :]
    abuf[:, 0:A_CARRY, :] = a_tail
    pbuf[:, 0:P_CARRY, :] = p_tail

    @pl.when(s == n_tiles - 1)
    def _():
        atail_ref[...] = _unslab(a_tail)
        ptail_ref[...] = _unslab(p_tail)


def _layer0(x2d, seq, gm, win, cw, cb, lng, lnb, pw_bd, ps, wout, a0, p0, gf, w1c, w2c, *, tile,
            l_off):
    n, d = x2d.shape
    assert seq % tile == 0 and n % seq == 0 and tile % min(ROW_BLOCK, tile) == 0
    n_tiles = n // tile
    kern = functools.partial(_layer0_kernel, tile=tile, l_off=l_off, n_tiles=n_tiles,
                             tiles_per_seq=seq // tile)
    consts = (gm, win, cw, cb, lng, lnb, pw_bd, ps, wout, a0, p0, gf, w1c, w2c)
    return pl.pallas_call(
        kern,
        grid=(n_tiles + 1,),
        in_specs=[pl.BlockSpec((tile, d), lambda s: (jnp.minimum(s, n_tiles - 1), 0))]
        + [_const_spec(c.shape) for c in consts],
        out_specs=[pl.BlockSpec((tile, d), lambda s: (jnp.maximum(s - 1, 0), 0)),
                   _const_spec((A_CARRY, CONV_DIM)), _const_spec((P_CARRY, POOL_DIM))],
        out_shape=[jax.ShapeDtypeStruct(x2d.shape, F32),
                   jax.ShapeDtypeStruct((A_CARRY, CONV_DIM), F32),
                   jax.ShapeDtypeStruct((P_CARRY, POOL_DIM), F32)],
        scratch_shapes=[pltpu.VMEM((N_SLAB, A_CARRY + tile, LANES), F32),
                        pltpu.VMEM((N_SLAB, P_CARRY + tile, LANES), F32),
                        pltpu.VMEM((tile, CONV_DIM + POOL_DIM), BF16),
                        pltpu.VMEM((tile, d), F32),
                        pltpu.VMEM((tile, d), BF16)],
        compiler_params=pltpu.CompilerParams(dimension_semantics=("arbitrary",),
                                             vmem_limit_bytes=VMEM_LIMIT),
        name="layer0",
    )(x2d, *consts)


def _gla_chunk(r0, z_scr, la_scr, og, state, tri_ref, hg_ref):
    rows = pl.ds(r0, CHUNK)
    cum = jnp.dot(tri_ref[...], la_scr[rows, :], preferred_element_type=F32,
                  precision=lax.Precision.HIGHEST)
    tot = cum[CHUNK - 1:CHUNK, :]
    kd = (z_scr[rows, GLA_DK:2 * GLA_DK] * jnp.exp(tot - cum)).astype(BF16)
    q = (z_scr[rows, 0:GLA_DK] * (GLA_HK ** -0.5)).astype(BF16)
    etot = jnp.exp(tot)
    glo = 2 * GLA_DK + GLA_DV
    for hh in range(GLA_HEADS):
        klo, vlo = hh * GLA_HK, 2 * GLA_DK + hh * GLA_HV
        v_h = z_scr[rows, vlo:vlo + GLA_HV].astype(BF16)
        upd = lax.dot_general(v_h, kd[:, klo:klo + GLA_HK], (((0,), (0,)), ((), ())),
                              preferred_element_type=F32)
        s_h = state[hh] * etot[:, klo:klo + GLA_HK] + upd
        state[hh] = s_h
        o = lax.dot_general(q[:, klo:klo + GLA_HK], s_h.astype(BF16), (((1,), (1,)), ((), ())),
                            preferred_element_type=F32)
        o = o * lax.rsqrt(jnp.mean(o * o, axis=-1, keepdims=True) + EPS) * hg_ref[...]
        gt = z_scr[rows, glo + hh * GLA_HV:glo + (hh + 1) * GLA_HV]
        og[rows, hh * GLA_HV:(hh + 1) * GLA_HV] = (o * (gt * _sigmoid(gt))).astype(BF16)


def _layer1_kernel(x_ref, gm_ref, wqkvg_ref, wr_ref, gw2_ref, gb_ref, hg_ref, wout_ref, s0_ref,
                   tri_ref, gf_ref, w1c_ref, w2c_ref, fg_ref, out_ref, sfin_ref,
                   z_scr, la_scr, og, state, hbuf, ubuf, *, tile, n_pad, n_tiles, tiles_per_seq,
                   final_norm):
    s = pl.program_id(0)
    t_in_seq = lax.rem(s, tiles_per_seq)

    @pl.when(s == 0)
    def _():
        og[...] = jnp.zeros(og.shape, og.dtype)
        hbuf[...] = jnp.zeros(hbuf.shape, hbuf.dtype)

    @pl.when(t_in_seq == 0)
    def _():
        state[...] = s0_ref[...]

    u = _rms(x_ref[...], gm_ref[...]).astype(BF16)
    z_scr[...] = jnp.dot(u, wqkvg_ref[...], preferred_element_type=F32)
    r = jnp.dot(u, wr_ref[...], preferred_element_type=F32)
    x = jnp.dot(r.astype(BF16), gw2_ref[...], preferred_element_type=F32) + gb_ref[...]
    la = (jnp.minimum(x, 0.0) - jnp.log1p(jnp.exp(-jnp.abs(x)))) * (1.0 / GLA_GATE_NORM)
    if n_pad:
        row = t_in_seq * tile + lax.broadcasted_iota(jnp.int32, (tile, GLA_DK), 0)
        la = jnp.where(row >= n_pad, la, 0.0)
    la_scr[...] = la

    h1 = hbuf[...] + jnp.dot(og[...], wout_ref[...], preferred_element_type=F32)
    out_ref[...] = h1
    ubuf[...] = _rms(h1, gf_ref[...]).astype(BF16)

    _paired_loop(
        tile // CHUNK,
        lambda c: _mlp_chunk(c, ubuf, w1c_ref, w2c_ref, out_ref),
        lambda i: _gla_chunk(_row_start(i, CHUNK), z_scr, la_scr, og, state, tri_ref, hg_ref))

    if final_norm:
        out_ref[...] = _rms(out_ref[...], fg_ref[...])
    hbuf[...] = x_ref[...]

    @pl.when(s == n_tiles - 1)
    def _():
        sfin_ref[...] = state[...]


def _layer1(x2d, seq, gm, wqkvg, wr, gw2, gb, hg, wout, s0, tri, gf, w1c, w2c, fg, *, tile, n_pad,
            final_norm):
    n, d = x2d.shape
    assert seq % tile == 0 and n % seq == 0 and tile % CHUNK == 0
    n_tiles = n // tile
    kern = functools.partial(_layer1_kernel, tile=tile, n_pad=n_pad, n_tiles=n_tiles,
                             tiles_per_seq=seq // tile, final_norm=final_norm)
    consts = (gm, wqkvg, wr, gw2, gb, hg, wout, s0, tri, gf, w1c, w2c, fg)
    state_shape = (GLA_HEADS, GLA_HV, GLA_HK)
    return pl.pallas_call(
        kern,
        grid=(n_tiles + 1,),
        in_specs=[pl.BlockSpec((tile, d), lambda s: (jnp.minimum(s, n_tiles - 1), 0))]
        + [_const_spec(c.shape) for c in consts],
        out_specs=[pl.BlockSpec((tile, d), lambda s: (jnp.maximum(s - 1, 0), 0)),
                   _const_spec(state_shape)],
        out_shape=[jax.ShapeDtypeStruct(x2d.shape, F32), jax.ShapeDtypeStruct(state_shape, F32)],
        scratch_shapes=[pltpu.VMEM((tile, N_QKVG), F32),
                        pltpu.VMEM((tile, GLA_DK), F32),
                        pltpu.VMEM((tile, GLA_DV), BF16),
                        pltpu.VMEM(state_shape, F32),
                        pltpu.VMEM((tile, d), F32),
                        pltpu.VMEM((tile, d), BF16)],
        compiler_params=pltpu.CompilerParams(dimension_semantics=("arbitrary",),
                                             vmem_limit_bytes=VMEM_LIMIT),
        name="layer1",
    )(x2d, *consts)


def kernel(x, meta_tokens, mix_norm_g, ffn_norm_g, ffn_w1, ffn_w2, cp_w_in, cp_conv_w, cp_conv_b,
           cp_ln_g, cp_ln_b, cp_pool_w, cp_pool_scale, cp_w_out, gla_w_in, gla_gate_w2, gla_gate_b,
           gla_head_g, gla_w_out, final_norm_g):
    bsz, seq, d = x.shape
    row = lambda v: v.reshape(1, -1).astype(F32)

    depth = ffn_w1.shape[0]
    w1c = ffn_w1.astype(BF16).reshape(depth, d, N_FF, FF_CHUNK).transpose(0, 2, 1, 3)
    w2c = ffn_w2.astype(BF16).reshape(depth, N_FF, FF_CHUNK, d)
    cp_win = cp_w_in[0].astype(BF16)
    cp_wout = cp_w_out[0].astype(BF16)
    pw_bd = jnp.zeros((POOL_DIM, POOL_DIM), F32)
    for gi in range(len(POOL_WINDOWS)):
        lo = gi * POOL_GROUP
        pw_bd = lax.dynamic_update_slice(pw_bd, cp_pool_w[0, gi], (lo, lo))
    pw_bd = pw_bd.astype(BF16)
    wqkvg = gla_w_in[0, :, :N_QKVG].astype(BF16)
    wr = jnp.pad(gla_w_in[0, :, N_QKVG:], ((0, 0), (0, LANES - GLA_GATE_RANK))).astype(BF16)
    gw2 = jnp.pad(gla_gate_w2[0], ((0, LANES - GLA_GATE_RANK), (0, 0))).astype(BF16)
    gla_wout = gla_w_out[0].astype(BF16)
    tri = jnp.tril(jnp.ones((CHUNK, CHUNK), F32))
    fg = row(final_norm_g)

    l0_args = (row(mix_norm_g[0]), cp_win, cp_conv_w[0].astype(F32), row(cp_conv_b[0]),
               row(cp_ln_g[0]), row(cp_ln_b[0]), pw_bd, row(cp_pool_scale[0]), cp_wout)
    l0_mlp = (row(ffn_norm_g[0]), w1c[0], w2c[0])
    l1_args = (row(mix_norm_g[1]), wqkvg, wr, gw2, row(gla_gate_b[0]), row(gla_head_g[0]), gla_wout)
    l1_mlp = (row(ffn_norm_g[1]), w1c[1], w2c[1], fg)

    hm, a_tail, p_tail = _layer0(meta_tokens.astype(F32), N_META, *l0_args,
                                 jnp.zeros((A_CARRY, CONV_DIM), F32),
                                 jnp.zeros((P_CARRY, POOL_DIM), F32), *l0_mlp, tile=N_META, l_off=0)
    hm = jnp.pad(hm, ((CHUNK - N_META, 0), (0, 0)))
    s_zero = jnp.zeros((GLA_HEADS, GLA_HV, GLA_HK), F32)
    _, s_meta = _layer1(hm, CHUNK, *l1_args, s_zero, tri, *l1_mlp, tile=CHUNK,
                        n_pad=CHUNK - N_META, final_norm=False)

    tile = 512
    h, _, _ = _layer0(x.astype(F32).reshape(bsz * seq, d), seq, *l0_args, a_tail, p_tail, *l0_mlp,
                      tile=tile, l_off=N_META)
    out, _ = _layer1(h, seq, *l1_args, s_meta, tri, *l1_mlp, tile=tile, n_pad=0, final_norm=True)
    return out.reshape(bsz, seq, d).astype(x.dtype)
```

```python
import functools

import jax
import jax.numpy as jnp
from jax import lax
from jax.experimental import pallas as pl
from jax.experimental.pallas import tpu as pltpu

D_MODEL = 1024
N_META = 16
CHUNK = 64
D_FF = 4 * D_MODEL
EPS = 1e-5

CONV_DIM = D_MODEL // 2
CONV_WIDTH = 31
POOL_DIM = D_MODEL // 2
POOL_WINDOWS = (2, 4, 8, 16)
POOL_GROUP = POOL_DIM // len(POOL_WINDOWS)

GLA_HEADS = 4
GLA_DK = D_MODEL // 2
GLA_DV = D_MODEL
GLA_HK = GLA_DK // GLA_HEADS
GLA_HV = GLA_DV // GLA_HEADS
GLA_GATE_RANK = 16
GLA_GATE_NORM = 16.0
N_QKVG = 2 * GLA_DK + 2 * GLA_DV

LANES = 128
N_SLAB = CONV_DIM // LANES
assert POOL_GROUP == LANES and POOL_DIM == CONV_DIM
A_CARRY = 32
P_CARRY = 16
ROW_BLOCK = 64
FF_CHUNK = 512
N_FF = D_FF // FF_CHUNK
VMEM_LIMIT = 56 * 1024 * 1024

F32 = jnp.float32
BF16 = jnp.bfloat16


def _rms(x, g):
    return x * lax.rsqrt(jnp.mean(x * x, axis=-1, keepdims=True) + EPS) * g


def _sigmoid(x):
    return 1.0 / (1.0 + jnp.exp(-x))


def _const_spec(shape):
    zeros = (0,) * len(shape)
    return pl.BlockSpec(shape, lambda *_: zeros)


def _tile_maps(n_tiles, pipelined):
    if not pipelined:
        return (lambda s: (s, 0)), (lambda s: (s, 0))
    return (lambda s: (jnp.minimum(s, n_tiles - 1), 0)), (lambda s: (jnp.maximum(s - 1, 0), 0))


def _dependent_zero(v):
    bits = lax.bitcast_convert_type(v, jnp.uint32)
    bits = lax.shift_right_logical(lax.shift_right_logical(bits, jnp.uint32(16)), jnp.uint32(16))
    return lax.bitcast_convert_type(bits, F32)


def _delayed(token, hops=3):
    for _ in range(hops):
        s = jnp.sum(token, axis=-1, keepdims=True)
        token = _dependent_zero(jnp.broadcast_to(s, token.shape))
    return token


def _slabs(x):
    return jnp.stack([x[:, k * LANES:(k + 1) * LANES] for k in range(N_SLAB)])


def _unslab(x):
    return jnp.concatenate([x[k] for k in range(N_SLAB)], axis=-1)


def _conv_pool_block(r0, rb, pos0, abuf, pbuf, mix, cw_ref, cb_ref, lng_ref, lnb_ref, l_off):
    conv = []
    for k in range(N_SLAB):
        cols = slice(k * LANES, (k + 1) * LANES)
        acc = jnp.broadcast_to(cb_ref[:, cols], (rb, LANES))
        for b in range(8):
            n = rb if b == 0 else rb + 8
            part = None
            for a in range(A_CARRY // 8 + 1):
                j = 8 * a + b - (A_CARRY - (CONV_WIDTH - 1))
                if 0 <= j < CONV_WIDTH:
                    term = cw_ref[j:j + 1, cols] * abuf[k, pl.ds(r0 + 8 * a, n), :]
                    part = term if part is None else part + term
            acc = acc + (part if b == 0 else pltpu.roll(part, n - b, axis=0)[0:rb])
        conv.append(acc)
    mu = sum(jnp.sum(c, axis=-1, keepdims=True) for c in conv) * (1.0 / CONV_DIM)
    cen = [c - mu for c in conv]
    var = sum(jnp.sum(c * c, axis=-1, keepdims=True) for c in cen) * (1.0 / CONV_DIM)
    rstd = lax.rsqrt(var + EPS)
    for k in range(N_SLAB):
        cols = slice(k * LANES, (k + 1) * LANES)
        y = cen[k] * rstd * lng_ref[:, cols] + lnb_ref[:, cols]
        mix[pl.ds(r0, rb), cols] = (y * _sigmoid(y)).astype(BF16)
    for k, w in enumerate(POOL_WINDOWS):
        win = pbuf[k, pl.ds(r0, P_CARRY + rb), :]
        cur = win[P_CARRY:]
        ws = win
        for step in range(w.bit_length() - 1):
            ws = ws + pltpu.roll(ws, 1 << step, axis=0)
        ws = ws[P_CARRY:]
        if l_off + 1 >= w:
            mean = ws * (1.0 / w)
        else:
            pos = pos0 + r0 + lax.broadcasted_iota(jnp.int32, (rb, LANES), 0)
            mean = ws / jnp.minimum(pos + 1, w).astype(F32)
        mix[pl.ds(r0, rb), CONV_DIM + k * LANES:CONV_DIM + (k + 1) * LANES] = (mean - cur).astype(BF16)


def _layer0_kernel(x_ref, gm_ref, win_ref, cw_ref, cb_ref, lng_ref, lnb_ref, pw_ref, ps_ref, wout_ref,
                   a0_ref, p0_ref, gf_ref, w1c_ref, w2c_ref, out_ref, atail_ref, ptail_ref,
                   abuf, pbuf, mix, hbuf, *, tile, l_off, n_tiles, tiles_per_seq, pipelined):
    s = pl.program_id(0)
    t_in_seq = lax.rem(s, tiles_per_seq)
    rb = min(ROW_BLOCK, tile)
    pos0 = l_off + t_in_seq * tile

    def conv_blocks():
        for i in range(tile // rb):
            _conv_pool_block(i * rb, rb, pos0, abuf, pbuf, mix, cw_ref, cb_ref, lng_ref, lnb_ref,
                             l_off)

    if pipelined:
        @pl.when(s == 0)
        def _():
            mix[...] = jnp.zeros(mix.shape, mix.dtype)
            hbuf[...] = jnp.zeros(hbuf.shape, hbuf.dtype)

    @pl.when(t_in_seq == 0)
    def _():
        abuf[:, 0:A_CARRY, :] = _slabs(a0_ref[...])
        pbuf[:, 0:P_CARRY, :] = _slabs(p0_ref[...])

    u = _rms(x_ref[...], gm_ref[...]).astype(BF16)
    z = jnp.dot(u, win_ref[...], preferred_element_type=F32)
    glu = z[:, :CONV_DIM] * _sigmoid(z[:, CONV_DIM:2 * CONV_DIM])
    pin = z[:, 2 * CONV_DIM:]
    if not pipelined:
        abuf[:, A_CARRY:A_CARRY + tile, :] = _slabs(glu)
        pbuf[:, P_CARRY:P_CARRY + tile, :] = _slabs(pin)
        conv_blocks()

    pooled = jnp.dot(mix[:, CONV_DIM:], pw_ref[...], preferred_element_type=F32) * ps_ref[...]
    y = jnp.dot(mix[:, 0:CONV_DIM], wout_ref[0:CONV_DIM, :], preferred_element_type=F32)
    y = y + jnp.dot(pooled.astype(BF16), wout_ref[CONV_DIM:, :], preferred_element_type=F32)
    h1 = (hbuf[...] if pipelined else x_ref[...]) + y
    uu = _rms(h1, gf_ref[...]).astype(BF16)
    acc = h1
    if pipelined:
        token = _dependent_zero(h1[0:8, 0:LANES])
        for i in range(tile // rb):
            rows = slice(i * rb, (i + 1) * rb)
            for k in range(N_SLAB):
                zero = jnp.tile(token, (rb // 8, 1))
                cols = slice(k * LANES, (k + 1) * LANES)
                abuf[k, A_CARRY + i * rb:A_CARRY + (i + 1) * rb, :] = glu[rows, cols] + zero
                pbuf[k, P_CARRY + i * rb:P_CARRY + (i + 1) * rb, :] = pin[rows, cols] + zero
                token = _delayed(token)
    for c in range(N_FF):
        a = jnp.maximum(jnp.dot(uu, w1c_ref[c], preferred_element_type=F32), 0.0)
        acc = acc + jnp.dot((a * a).astype(BF16), w2c_ref[c], preferred_element_type=F32)
    out_ref[...] = acc
    if pipelined:
        conv_blocks()
        hbuf[...] = x_ref[...]

    a_tail = abuf[:, tile:tile + A_CARRY, :]
    p_tail = pbuf[:, tile:tile + P_CARRY, :]
    abuf[:, 0:A_CARRY, :] = a_tail
    pbuf[:, 0:P_CARRY, :] = p_tail

    @pl.when(s == n_tiles - 1)
    def _():
        atail_ref[...] = _unslab(a_tail)
        ptail_ref[...] = _unslab(p_tail)


def _layer0(x2d, seq, gm, win, cw, cb, lng, lnb, pw_bd, ps, wout, a0, p0, gf, w1c, w2c, *, tile,
            l_off):
    n, d = x2d.shape
    assert seq % tile == 0 and n % seq == 0 and tile % min(ROW_BLOCK, tile) == 0
    n_tiles = n // tile
    pipelined = n_tiles > 1
    kern = functools.partial(_layer0_kernel, tile=tile, l_off=l_off, n_tiles=n_tiles,
                             tiles_per_seq=seq // tile, pipelined=pipelined)
    consts = (gm, win, cw, cb, lng, lnb, pw_bd, ps, wout, a0, p0, gf, w1c, w2c)
    in_map, out_map = _tile_maps(n_tiles, pipelined)
    return pl.pallas_call(
        kern,
        grid=(n_tiles + pipelined,),
        in_specs=[pl.BlockSpec((tile, d), in_map)] + [_const_spec(c.shape) for c in consts],
        out_specs=[pl.BlockSpec((tile, d), out_map),
                   _const_spec((A_CARRY, CONV_DIM)), _const_spec((P_CARRY, POOL_DIM))],
        out_shape=[jax.ShapeDtypeStruct(x2d.shape, F32),
                   jax.ShapeDtypeStruct((A_CARRY, CONV_DIM), F32),
                   jax.ShapeDtypeStruct((P_CARRY, POOL_DIM), F32)],
        scratch_shapes=[pltpu.VMEM((N_SLAB, A_CARRY + tile, LANES), F32),
                        pltpu.VMEM((N_SLAB, P_CARRY + tile, LANES), F32),
                        pltpu.VMEM((tile, CONV_DIM + POOL_DIM), BF16),
                        pltpu.VMEM((tile, d), F32)],
        compiler_params=pltpu.CompilerParams(dimension_semantics=("arbitrary",),
                                             vmem_limit_bytes=VMEM_LIMIT),
        name="layer0",
    )(x2d, *consts)


def _gla_update(r0, z_scr, la_scr, state, tri_ref):
    rows = pl.ds(r0, CHUNK)
    cum = jnp.dot(tri_ref[...], la_scr[rows, :], preferred_element_type=F32,
                  precision=lax.Precision.HIGHEST)
    tot = cum[CHUNK - 1:CHUNK, :]
    kd = (z_scr[rows, GLA_DK:2 * GLA_DK] * jnp.exp(tot - cum)).astype(BF16)
    etot = jnp.exp(tot)
    for hh in range(GLA_HEADS):
        klo, vlo = hh * GLA_HK, 2 * GLA_DK + hh * GLA_HV
        v_h = z_scr[rows, vlo:vlo + GLA_HV].astype(BF16)
        upd = lax.dot_general(v_h, kd[:, klo:klo + GLA_HK], (((0,), (0,)), ((), ())),
                              preferred_element_type=F32)
        state[hh] = state[hh] * etot[:, klo:klo + GLA_HK] + upd


def _gla_output(r0, z_scr, og, state, hg_ref):
    rows = pl.ds(r0, CHUNK)
    q = (z_scr[rows, 0:GLA_DK] * (GLA_HK ** -0.5)).astype(BF16)
    glo = 2 * GLA_DK + GLA_DV
    for hh in range(GLA_HEADS):
        klo = hh * GLA_HK
        o = lax.dot_general(q[:, klo:klo + GLA_HK], state[hh].astype(BF16),
                            (((1,), (1,)), ((), ())), preferred_element_type=F32)
        o = o * lax.rsqrt(jnp.mean(o * o, axis=-1, keepdims=True) + EPS) * hg_ref[...]
        gt = z_scr[rows, glo + hh * GLA_HV:glo + (hh + 1) * GLA_HV]
        og[rows, hh * GLA_HV:(hh + 1) * GLA_HV] = (o * (gt * _sigmoid(gt))).astype(BF16)


def _layer1_kernel(x_ref, gm_ref, wqkvg_ref, wr_ref, gw2_ref, gb_ref, hg_ref, wout_ref, s0_ref,
                   tri_ref, gf_ref, w1c_ref, w2c_ref, fg_ref, out_ref, sfin_ref,
                   z_scr, la_scr, og, state, hbuf, *, tile, n_pad, n_tiles, tiles_per_seq,
                   final_norm, state_only):
    s = pl.program_id(0)
    t_in_seq = lax.rem(s, tiles_per_seq)
    n_chunks = tile // CHUNK

    if not state_only:
        @pl.when(s == 0)
        def _():
            og[...] = jnp.zeros(og.shape, og.dtype)
            hbuf[...] = jnp.zeros(hbuf.shape, hbuf.dtype)

    @pl.when(t_in_seq == 0)
    def _():
        state[...] = s0_ref[...]

    u = _rms(x_ref[...], gm_ref[...]).astype(BF16)
    z_scr[...] = jnp.dot(u, wqkvg_ref[...], preferred_element_type=F32)
    r = jnp.dot(u, wr_ref[...], preferred_element_type=F32)
    x = jnp.dot(r.astype(BF16), gw2_ref[...], preferred_element_type=F32) + gb_ref[...]
    la = (jnp.minimum(x, 0.0) - jnp.log1p(jnp.exp(-jnp.abs(x)))) * (1.0 / GLA_GATE_NORM)
    if n_pad:
        row = t_in_seq * tile + lax.broadcasted_iota(jnp.int32, (tile, GLA_DK), 0)
        la = jnp.where(row >= n_pad, la, 0.0)
    la_scr[...] = la

    if state_only:
        for c in range(n_chunks):
            _gla_update(c * CHUNK, z_scr, la_scr, state, tri_ref)
        out_ref[...] = x_ref[...]
    else:
        h1 = hbuf[...] + jnp.dot(og[...], wout_ref[...], preferred_element_type=F32)
        uu = _rms(h1, gf_ref[...]).astype(BF16)
        acc = h1
        for c in range(max(N_FF, n_chunks)):
            if c < n_chunks:
                _gla_update(c * CHUNK, z_scr, la_scr, state, tri_ref)
            if c < N_FF:
                a = jnp.maximum(jnp.dot(uu, w1c_ref[c], preferred_element_type=F32), 0.0)
            if c < n_chunks:
                _gla_output(c * CHUNK, z_scr, og, state, hg_ref)
            if c < N_FF:
                acc = acc + jnp.dot((a * a).astype(BF16), w2c_ref[c], preferred_element_type=F32)
        if final_norm:
            acc = _rms(acc, fg_ref[...])
        out_ref[...] = acc
        hbuf[...] = x_ref[...]

    @pl.when(s == n_tiles - 1)
    def _():
        sfin_ref[...] = state[...]


def _layer1(x2d, seq, gm, wqkvg, wr, gw2, gb, hg, wout, s0, tri, gf, w1c, w2c, fg, *, tile, n_pad,
            final_norm, state_only=False):
    n, d = x2d.shape
    assert seq % tile == 0 and n % seq == 0 and tile % CHUNK == 0
    n_tiles = n // tile
    pipelined = not state_only
    kern = functools.partial(_layer1_kernel, tile=tile, n_pad=n_pad, n_tiles=n_tiles,
                             tiles_per_seq=seq // tile, final_norm=final_norm, state_only=state_only)
    consts = (gm, wqkvg, wr, gw2, gb, hg, wout, s0, tri, gf, w1c, w2c, fg)
    state_shape = (GLA_HEADS, GLA_HV, GLA_HK)
    in_map, out_map = _tile_maps(n_tiles, pipelined)
    return pl.pallas_call(
        kern,
        grid=(n_tiles + pipelined,),
        in_specs=[pl.BlockSpec((tile, d), in_map)] + [_const_spec(c.shape) for c in consts],
        out_specs=[pl.BlockSpec((tile, d), out_map), _const_spec(state_shape)],
        out_shape=[jax.ShapeDtypeStruct(x2d.shape, F32), jax.ShapeDtypeStruct(state_shape, F32)],
        scratch_shapes=[pltpu.VMEM((tile, N_QKVG), F32),
                        pltpu.VMEM((tile, GLA_DK), F32),
                        pltpu.VMEM((tile, GLA_DV), BF16),
                        pltpu.VMEM(state_shape, F32),
                        pltpu.VMEM((tile, d), F32)],
        compiler_params=pltpu.CompilerParams(dimension_semantics=("arbitrary",),
                                             vmem_limit_bytes=VMEM_LIMIT),
        name="layer1",
    )(x2d, *consts)


def kernel(x, meta_tokens, mix_norm_g, ffn_norm_g, ffn_w1, ffn_w2, cp_w_in, cp_conv_w, cp_conv_b,
           cp_ln_g, cp_ln_b, cp_pool_w, cp_pool_scale, cp_w_out, gla_w_in, gla_gate_w2, gla_gate_b,
           gla_head_g, gla_w_out, final_norm_g):
    bsz, seq, d = x.shape
    row = lambda v: v.reshape(1, -1).astype(F32)

    depth = ffn_w1.shape[0]
    w1c = ffn_w1.astype(BF16).reshape(depth, d, N_FF, FF_CHUNK).transpose(0, 2, 1, 3)
    w2c = ffn_w2.astype(BF16).reshape(depth, N_FF, FF_CHUNK, d)
    cp_win = cp_w_in[0].astype(BF16)
    cp_wout = cp_w_out[0].astype(BF16)
    pw_bd = jnp.zeros((POOL_DIM, POOL_DIM), F32)
    for gi in range(len(POOL_WINDOWS)):
        lo = gi * POOL_GROUP
        pw_bd = lax.dynamic_update_slice(pw_bd, cp_pool_w[0, gi], (lo, lo))
    pw_bd = pw_bd.astype(BF16)
    wqkvg = gla_w_in[0, :, :N_QKVG].astype(BF16)
    wr = jnp.pad(gla_w_in[0, :, N_QKVG:], ((0, 0), (0, LANES - GLA_GATE_RANK))).astype(BF16)
    gw2 = jnp.pad(gla_gate_w2[0], ((0, LANES - GLA_GATE_RANK), (0, 0))).astype(BF16)
    gla_wout = gla_w_out[0].astype(BF16)
    tri = jnp.tril(jnp.ones((CHUNK, CHUNK), F32))
    fg = row(final_norm_g)

    l0_args = (row(mix_norm_g[0]), cp_win, cp_conv_w[0].astype(F32), row(cp_conv_b[0]),
               row(cp_ln_g[0]), row(cp_ln_b[0]), pw_bd, row(cp_pool_scale[0]), cp_wout)
    l0_mlp = (row(ffn_norm_g[0]), w1c[0], w2c[0])
    l1_args = (row(mix_norm_g[1]), wqkvg, wr, gw2, row(gla_gate_b[0]), row(gla_head_g[0]), gla_wout)
    l1_mlp = (row(ffn_norm_g[1]), w1c[1], w2c[1], fg)

    hm, a_tail, p_tail = _layer0(meta_tokens.astype(F32), N_META, *l0_args,
                                 jnp.zeros((A_CARRY, CONV_DIM), F32),
                                 jnp.zeros((P_CARRY, POOL_DIM), F32), *l0_mlp, tile=N_META, l_off=0)
    hm = jnp.pad(hm, ((CHUNK - N_META, 0), (0, 0)))
    s_zero = jnp.zeros((GLA_HEADS, GLA_HV, GLA_HK), F32)
    _, s_meta = _layer1(hm, CHUNK, *l1_args, s_zero, tri, *l1_mlp, tile=CHUNK,
                        n_pad=CHUNK - N_META, final_norm=False, state_only=True)

    tile = 512
    h, _, _ = _layer0(x.astype(F32).reshape(bsz * seq, d), seq, *l0_args, a_tail, p_tail, *l0_mlp,
                      tile=tile, l_off=N_META)
    out, _ = _layer1(h, seq, *l1_args, s_meta, tri, *l1_mlp, tile=tile, n_pad=0, final_norm=True)
    return out.reshape(bsz, seq, d).astype(x.dtype)
```

```python
import functools

import jax
import jax.numpy as jnp
from jax import lax
from jax.experimental import pallas as pl
from jax.experimental.pallas import tpu as pltpu

D_MODEL = 1024
N_META = 16
CHUNK = 64
D_FF = 4 * D_MODEL
EPS = 1e-5

CONV_DIM = D_MODEL // 2
CONV_WIDTH = 31
POOL_DIM = D_MODEL // 2
POOL_WINDOWS = (2, 4, 8, 16)
POOL_GROUP = POOL_DIM // len(POOL_WINDOWS)

GLA_HEADS = 4
GLA_DK = D_MODEL // 2
GLA_DV = D_MODEL
GLA_HK = GLA_DK // GLA_HEADS
GLA_HV = GLA_DV // GLA_HEADS
GLA_GATE_RANK = 16
GLA_GATE_NORM = 16.0
N_QKVG = 2 * GLA_DK + 2 * GLA_DV

LANES = 128
N_SLAB = CONV_DIM // LANES
assert POOL_GROUP == LANES and POOL_DIM == CONV_DIM
A_CARRY = 32
P_CARRY = 16
ROW_BLOCK = 64
FF_CHUNK = 512
N_FF = D_FF // FF_CHUNK
VMEM_LIMIT = 56 * 1024 * 1024

F32 = jnp.float32
BF16 = jnp.bfloat16


def _rms(x, g):
    return x * lax.rsqrt(jnp.mean(x * x, axis=-1, keepdims=True) + EPS) * g


def _sigmoid(x):
    return 1.0 / (1.0 + jnp.exp(-x))


def _const_spec(shape):
    zeros = (0,) * len(shape)
    return pl.BlockSpec(shape, lambda *_: zeros)


class _Layer:
    def __init__(self, array, layer):
        self.array, self.layer = array, layer
        self.shape = array.shape[1:]

    def spec(self):
        index = (self.layer,) + (0,) * len(self.shape)
        return pl.BlockSpec((None,) + self.shape, lambda *_: index, pipeline_mode=pl.Buffered(1))


def _split_consts(consts):
    arrays = [c.array if isinstance(c, _Layer) else c for c in consts]
    specs = [c.spec() if isinstance(c, _Layer) else _const_spec(c.shape) for c in consts]
    return arrays, specs


def _cast_kernel(x_ref, o_ref):
    o_ref[...] = x_ref[...].astype(o_ref.dtype)


def _to_bf16(w, block, cols=None):
    layers, rows, all_cols = w.shape
    cols = all_cols if cols is None else cols
    br, bc = block
    assert rows % br == 0 and cols % bc == 0
    spec = pl.BlockSpec((1, br, bc), lambda l, i, j: (l, i, j))
    return pl.pallas_call(
        _cast_kernel,
        grid=(layers, rows // br, cols // bc),
        in_specs=[spec],
        out_specs=spec,
        out_shape=jax.ShapeDtypeStruct((layers, rows, cols), BF16),
        name="to_bf16",
    )(w)


def _tile_maps(n_tiles, pipelined):
    if not pipelined:
        return (lambda s: (s, 0)), (lambda s: (s, 0))
    return (lambda s: (jnp.minimum(s, n_tiles - 1), 0)), (lambda s: (jnp.maximum(s - 1, 0), 0))


def _dependent_zero(v):
    bits = lax.bitcast_convert_type(v, jnp.uint32)
    bits = lax.shift_right_logical(lax.shift_right_logical(bits, jnp.uint32(16)), jnp.uint32(16))
    return lax.bitcast_convert_type(bits, F32)


def _delayed(token, hops=3):
    for _ in range(hops):
        s = jnp.sum(token, axis=-1, keepdims=True)
        token = _dependent_zero(jnp.broadcast_to(s, token.shape))
    return token


def _slabs(x):
    return jnp.stack([x[:, k * LANES:(k + 1) * LANES] for k in range(N_SLAB)])


def _unslab(x):
    return jnp.concatenate([x[k] for k in range(N_SLAB)], axis=-1)


def _conv_pool_block(r0, rb, pos0, abuf, pbuf, mix, cw_ref, cb_ref, lng_ref, lnb_ref, l_off):
    conv = []
    for k in range(N_SLAB):
        cols = slice(k * LANES, (k + 1) * LANES)
        acc = jnp.broadcast_to(cb_ref[:, cols], (rb, LANES))
        for b in range(8):
            n = rb if b == 0 else rb + 8
            part = None
            for a in range(A_CARRY // 8 + 1):
                j = 8 * a + b - (A_CARRY - (CONV_WIDTH - 1))
                if 0 <= j < CONV_WIDTH:
                    term = cw_ref[j:j + 1, cols] * abuf[k, pl.ds(r0 + 8 * a, n), :]
                    part = term if part is None else part + term
            acc = acc + (part if b == 0 else pltpu.roll(part, n - b, axis=0)[0:rb])
        conv.append(acc)
    mu = sum(jnp.sum(c, axis=-1, keepdims=True) for c in conv) * (1.0 / CONV_DIM)
    cen = [c - mu for c in conv]
    var = sum(jnp.sum(c * c, axis=-1, keepdims=True) for c in cen) * (1.0 / CONV_DIM)
    rstd = lax.rsqrt(var + EPS)
    for k in range(N_SLAB):
        cols = slice(k * LANES, (k + 1) * LANES)
        y = cen[k] * rstd * lng_ref[:, cols] + lnb_ref[:, cols]
        mix[pl.ds(r0, rb), cols] = (y * _sigmoid(y)).astype(BF16)
    for k, w in enumerate(POOL_WINDOWS):
        win = pbuf[k, pl.ds(r0, P_CARRY + rb), :]
        cur = win[P_CARRY:]
        ws = win
        for step in range(w.bit_length() - 1):
            ws = ws + pltpu.roll(ws, 1 << step, axis=0)
        ws = ws[P_CARRY:]
        if l_off + 1 >= w:
            mean = ws * (1.0 / w)
        else:
            pos = pos0 + r0 + lax.broadcasted_iota(jnp.int32, (rb, LANES), 0)
            mean = ws / jnp.minimum(pos + 1, w).astype(F32)
        mix[pl.ds(r0, rb), CONV_DIM + k * LANES:CONV_DIM + (k + 1) * LANES] = (mean - cur).astype(BF16)


def _layer0_kernel(x_ref, gm_ref, win_ref, cw_ref, cb_ref, lng_ref, lnb_ref, pw_ref, ps_ref, wout_ref,
                   a0_ref, p0_ref, gf_ref, w1_ref, w2c_ref, out_ref, atail_ref, ptail_ref,
                   abuf, pbuf, mix, hbuf, *, tile, l_off, n_tiles, tiles_per_seq, pipelined):
    s = pl.program_id(0)
    t_in_seq = lax.rem(s, tiles_per_seq)
    rb = min(ROW_BLOCK, tile)
    pos0 = l_off + t_in_seq * tile

    def conv_blocks():
        for i in range(tile // rb):
            _conv_pool_block(i * rb, rb, pos0, abuf, pbuf, mix, cw_ref, cb_ref, lng_ref, lnb_ref,
                             l_off)

    if pipelined:
        @pl.when(s == 0)
        def _():
            mix[...] = jnp.zeros(mix.shape, mix.dtype)
            hbuf[...] = jnp.zeros(hbuf.shape, hbuf.dtype)

    @pl.when(t_in_seq == 0)
    def _():
        abuf[:, 0:A_CARRY, :] = _slabs(a0_ref[...])
        pbuf[:, 0:P_CARRY, :] = _slabs(p0_ref[...])

    u = _rms(x_ref[...], gm_ref[...]).astype(BF16)
    z = jnp.dot(u, win_ref[...], preferred_element_type=F32)
    glu = z[:, :CONV_DIM] * _sigmoid(z[:, CONV_DIM:2 * CONV_DIM])
    pin = z[:, 2 * CONV_DIM:]
    if not pipelined:
        abuf[:, A_CARRY:A_CARRY + tile, :] = _slabs(glu)
        pbuf[:, P_CARRY:P_CARRY + tile, :] = _slabs(pin)
        conv_blocks()

    pooled = jnp.dot(mix[:, CONV_DIM:], pw_ref[...], preferred_element_type=F32) * ps_ref[...]
    y = jnp.dot(mix[:, 0:CONV_DIM], wout_ref[0:CONV_DIM, :], preferred_element_type=F32)
    y = y + jnp.dot(pooled.astype(BF16), wout_ref[CONV_DIM:, :], preferred_element_type=F32)
    h1 = (hbuf[...] if pipelined else x_ref[...]) + y
    uu = _rms(h1, gf_ref[...]).astype(BF16)
    acc = h1
    if pipelined:
        token = _dependent_zero(h1[0:8, 0:LANES])
        for i in range(tile // rb):
            rows = slice(i * rb, (i + 1) * rb)
            for k in range(N_SLAB):
                zero = jnp.tile(token, (rb // 8, 1))
                cols = slice(k * LANES, (k + 1) * LANES)
                abuf[k, A_CARRY + i * rb:A_CARRY + (i + 1) * rb, :] = glu[rows, cols] + zero
                pbuf[k, P_CARRY + i * rb:P_CARRY + (i + 1) * rb, :] = pin[rows, cols] + zero
                token = _delayed(token)
    for c in range(N_FF):
        a = jnp.maximum(jnp.dot(uu, w1_ref[:, c * FF_CHUNK:(c + 1) * FF_CHUNK],
                                    preferred_element_type=F32), 0.0)
        acc = acc + jnp.dot((a * a).astype(BF16), w2c_ref[c], preferred_element_type=F32)
    out_ref[...] = acc
    if pipelined:
        conv_blocks()
        hbuf[...] = x_ref[...]

    a_tail = abuf[:, tile:tile + A_CARRY, :]
    p_tail = pbuf[:, tile:tile + P_CARRY, :]
    abuf[:, 0:A_CARRY, :] = a_tail
    pbuf[:, 0:P_CARRY, :] = p_tail

    @pl.when(s == n_tiles - 1)
    def _():
        atail_ref[...] = _unslab(a_tail)
        ptail_ref[...] = _unslab(p_tail)


def _layer0(x2d, seq, gm, win, cw, cb, lng, lnb, pw_bd, ps, wout, a0, p0, gf, w1, w2c, *, tile,
            l_off):
    n, d = x2d.shape
    assert seq % tile == 0 and n % seq == 0 and tile % min(ROW_BLOCK, tile) == 0
    n_tiles = n // tile
    pipelined = n_tiles > 1
    kern = functools.partial(_layer0_kernel, tile=tile, l_off=l_off, n_tiles=n_tiles,
                             tiles_per_seq=seq // tile, pipelined=pipelined)
    consts, const_specs = _split_consts((gm, win, cw, cb, lng, lnb, pw_bd, ps, wout, a0, p0, gf, w1,
                                         w2c))
    in_map, out_map = _tile_maps(n_tiles, pipelined)
    return pl.pallas_call(
        kern,
        grid=(n_tiles + pipelined,),
        in_specs=[pl.BlockSpec((tile, d), in_map)] + const_specs,
        out_specs=[pl.BlockSpec((tile, d), out_map),
                   _const_spec((A_CARRY, CONV_DIM)), _const_spec((P_CARRY, POOL_DIM))],
        out_shape=[jax.ShapeDtypeStruct(x2d.shape, F32),
                   jax.ShapeDtypeStruct((A_CARRY, CONV_DIM), F32),
                   jax.ShapeDtypeStruct((P_CARRY, POOL_DIM), F32)],
        scratch_shapes=[pltpu.VMEM((N_SLAB, A_CARRY + tile, LANES), F32),
                        pltpu.VMEM((N_SLAB, P_CARRY + tile, LANES), F32),
                        pltpu.VMEM((tile, CONV_DIM + POOL_DIM), BF16),
                        pltpu.VMEM((tile, d), F32)],
        compiler_params=pltpu.CompilerParams(dimension_semantics=("arbitrary",),
                                             vmem_limit_bytes=VMEM_LIMIT),
        name="layer0",
    )(x2d, *consts)


def _gla_update(r0, z_scr, la_scr, state, tri_ref):
    rows = pl.ds(r0, CHUNK)
    cum = jnp.dot(tri_ref[...], la_scr[rows, :], preferred_element_type=F32,
                  precision=lax.Precision.HIGHEST)
    tot = cum[CHUNK - 1:CHUNK, :]
    kd = (z_scr[rows, GLA_DK:2 * GLA_DK] * jnp.exp(tot - cum)).astype(BF16)
    etot = jnp.exp(tot)
    for hh in range(GLA_HEADS):
        klo, vlo = hh * GLA_HK, 2 * GLA_DK + hh * GLA_HV
        v_h = z_scr[rows, vlo:vlo + GLA_HV].astype(BF16)
        upd = lax.dot_general(v_h, kd[:, klo:klo + GLA_HK], (((0,), (0,)), ((), ())),
                              preferred_element_type=F32)
        state[hh] = state[hh] * etot[:, klo:klo + GLA_HK] + upd


def _gla_output(r0, z_scr, og, state, hg_ref):
    rows = pl.ds(r0, CHUNK)
    q = (z_scr[rows, 0:GLA_DK] * (GLA_HK ** -0.5)).astype(BF16)
    glo = 2 * GLA_DK + GLA_DV
    for hh in range(GLA_HEADS):
        klo = hh * GLA_HK
        o = lax.dot_general(q[:, klo:klo + GLA_HK], state[hh].astype(BF16),
                            (((1,), (1,)), ((), ())), preferred_element_type=F32)
        o = o * lax.rsqrt(jnp.mean(o * o, axis=-1, keepdims=True) + EPS) * hg_ref[...]
        gt = z_scr[rows, glo + hh * GLA_HV:glo + (hh + 1) * GLA_HV]
        og[rows, hh * GLA_HV:(hh + 1) * GLA_HV] = (o * (gt * _sigmoid(gt))).astype(BF16)


def _layer1_kernel(x_ref, gm_ref, wqkvg_ref, wr_ref, gw2_ref, gb_ref, hg_ref, wout_ref, s0_ref,
                   tri_ref, gf_ref, w1_ref, w2c_ref, fg_ref, out_ref, sfin_ref,
                   z_scr, la_scr, og, state, hbuf, *, tile, n_pad, n_tiles, tiles_per_seq,
                   final_norm, state_only):
    s = pl.program_id(0)
    t_in_seq = lax.rem(s, tiles_per_seq)
    n_chunks = tile // CHUNK

    if not state_only:
        @pl.when(s == 0)
        def _():
            og[...] = jnp.zeros(og.shape, og.dtype)
            hbuf[...] = jnp.zeros(hbuf.shape, hbuf.dtype)

    @pl.when(t_in_seq == 0)
    def _():
        state[...] = s0_ref[...]

    u = _rms(x_ref[...], gm_ref[...]).astype(BF16)
    z_scr[...] = jnp.dot(u, wqkvg_ref[...], preferred_element_type=F32)
    r = jnp.dot(u, wr_ref[...], preferred_element_type=F32)
    x = jnp.dot(r.astype(BF16), gw2_ref[...], preferred_element_type=F32) + gb_ref[...]
    la = (jnp.minimum(x, 0.0) - jnp.log1p(jnp.exp(-jnp.abs(x)))) * (1.0 / GLA_GATE_NORM)
    if n_pad:
        row = t_in_seq * tile + lax.broadcasted_iota(jnp.int32, (tile, GLA_DK), 0)
        la = jnp.where(row >= n_pad, la, 0.0)
    la_scr[...] = la

    if state_only:
        for c in range(n_chunks):
            _gla_update(c * CHUNK, z_scr, la_scr, state, tri_ref)
        out_ref[...] = x_ref[...]
    else:
        h1 = hbuf[...] + jnp.dot(og[...], wout_ref[...], preferred_element_type=F32)
        uu = _rms(h1, gf_ref[...]).astype(BF16)
        acc = h1
        for c in range(max(N_FF, n_chunks)):
            if c < n_chunks:
                _gla_update(c * CHUNK, z_scr, la_scr, state, tri_ref)
            if c < N_FF:
                a = jnp.maximum(jnp.dot(uu, w1_ref[:, c * FF_CHUNK:(c + 1) * FF_CHUNK],
                                    preferred_element_type=F32), 0.0)
            if c < n_chunks:
                _gla_output(c * CHUNK, z_scr, og, state, hg_ref)
            if c < N_FF:
                acc = acc + jnp.dot((a * a).astype(BF16), w2c_ref[c], preferred_element_type=F32)
        if final_norm:
            acc = _rms(acc, fg_ref[...])
        out_ref[...] = acc
        hbuf[...] = x_ref[...]

    @pl.when(s == n_tiles - 1)
    def _():
        sfin_ref[...] = state[...]


def _layer1(x2d, seq, gm, wqkvg, wr, gw2, gb, hg, wout, s0, tri, gf, w1, w2c, fg, *, tile, n_pad,
            final_norm, state_only=False):
    n, d = x2d.shape
    assert seq % tile == 0 and n % seq == 0 and tile % CHUNK == 0
    n_tiles = n // tile
    pipelined = not state_only
    kern = functools.partial(_layer1_kernel, tile=tile, n_pad=n_pad, n_tiles=n_tiles,
                             tiles_per_seq=seq // tile, final_norm=final_norm, state_only=state_only)
    consts, const_specs = _split_consts((gm, wqkvg, wr, gw2, gb, hg, wout, s0, tri, gf, w1, w2c, fg))
    state_shape = (GLA_HEADS, GLA_HV, GLA_HK)
    in_map, out_map = _tile_maps(n_tiles, pipelined)
    return pl.pallas_call(
        kern,
        grid=(n_tiles + pipelined,),
        in_specs=[pl.BlockSpec((tile, d), in_map)] + const_specs,
        out_specs=[pl.BlockSpec((tile, d), out_map), _const_spec(state_shape)],
        out_shape=[jax.ShapeDtypeStruct(x2d.shape, F32), jax.ShapeDtypeStruct(state_shape, F32)],
        scratch_shapes=[pltpu.VMEM((tile, N_QKVG), F32),
                        pltpu.VMEM((tile, GLA_DK), F32),
                        pltpu.VMEM((tile, GLA_DV), BF16),
                        pltpu.VMEM(state_shape, F32),
                        pltpu.VMEM((tile, d), F32)],
        compiler_params=pltpu.CompilerParams(dimension_semantics=("arbitrary",),
                                             vmem_limit_bytes=VMEM_LIMIT),
        name="layer1",
    )(x2d, *consts)


def kernel(x, meta_tokens, mix_norm_g, ffn_norm_g, ffn_w1, ffn_w2, cp_w_in, cp_conv_w, cp_conv_b,
           cp_ln_g, cp_ln_b, cp_pool_w, cp_pool_scale, cp_w_out, gla_w_in, gla_gate_w2, gla_gate_b,
           gla_head_g, gla_w_out, final_norm_g):
    bsz, seq, d = x.shape
    row = lambda v: v.reshape(1, -1).astype(F32)

    depth = ffn_w1.shape[0]
    w1 = _to_bf16(ffn_w1, (d, 2 * FF_CHUNK))
    w2c = _to_bf16(ffn_w2, (2 * FF_CHUNK, d)).reshape(depth, N_FF, FF_CHUNK, d)
    cp_win = _Layer(_to_bf16(cp_w_in, (d, FF_CHUNK)), 0)
    cp_wout = _Layer(_to_bf16(cp_w_out, (d, FF_CHUNK)), 0)
    pw_bd = jnp.zeros((POOL_DIM, POOL_DIM), F32)
    for gi in range(len(POOL_WINDOWS)):
        lo = gi * POOL_GROUP
        pw_bd = lax.dynamic_update_slice(pw_bd, cp_pool_w[0, gi], (lo, lo))
    pw_bd = pw_bd.astype(BF16)
    wqkvg = _Layer(_to_bf16(gla_w_in, (d, FF_CHUNK), cols=N_QKVG), 0)
    wr = jnp.pad(gla_w_in[0, :, N_QKVG:], ((0, 0), (0, LANES - GLA_GATE_RANK))).astype(BF16)
    gw2 = jnp.pad(gla_gate_w2[0], ((0, LANES - GLA_GATE_RANK), (0, 0))).astype(BF16)
    gla_wout = _Layer(_to_bf16(gla_w_out, (d, FF_CHUNK)), 0)
    tri = jnp.tril(jnp.ones((CHUNK, CHUNK), F32))
    fg = row(final_norm_g)

    l0_args = (row(mix_norm_g[0]), cp_win, cp_conv_w[0].astype(F32), row(cp_conv_b[0]),
               row(cp_ln_g[0]), row(cp_ln_b[0]), pw_bd, row(cp_pool_scale[0]), cp_wout)
    l0_mlp = (row(ffn_norm_g[0]), _Layer(w1, 0), _Layer(w2c, 0))
    l1_args = (row(mix_norm_g[1]), wqkvg, wr, gw2, row(gla_gate_b[0]), row(gla_head_g[0]), gla_wout)
    l1_mlp = (row(ffn_norm_g[1]), _Layer(w1, 1), _Layer(w2c, 1), fg)

    hm, a_tail, p_tail = _layer0(meta_tokens.astype(F32), N_META, *l0_args,
                                 jnp.zeros((A_CARRY, CONV_DIM), F32),
                                 jnp.zeros((P_CARRY, POOL_DIM), F32), *l0_mlp, tile=N_META, l_off=0)
    hm = jnp.pad(hm, ((CHUNK - N_META, 0), (0, 0)))
    s_zero = jnp.zeros((GLA_HEADS, GLA_HV, GLA_HK), F32)
    _, s_meta = _layer1(hm, CHUNK, *l1_args, s_zero, tri, *l1_mlp, tile=CHUNK,
                        n_pad=CHUNK - N_META, final_norm=False, state_only=True)

    tile = 512
    h, _, _ = _layer0(x.astype(F32).reshape(bsz * seq, d), seq, *l0_args, a_tail, p_tail, *l0_mlp,
                      tile=tile, l_off=N_META)
    out, _ = _layer1(h, seq, *l1_args, s_meta, tri, *l1_mlp, tile=tile, n_pad=0, final_norm=True)
    return out.reshape(bsz, seq, d).astype(x.dtype)
```

```python
import functools

import jax
import jax.numpy as jnp
from jax import lax
from jax.experimental import pallas as pl
from jax.experimental.pallas import tpu as pltpu

D_MODEL = 1024
N_META = 16
CHUNK = 64
D_FF = 4 * D_MODEL
EPS = 1e-5

CONV_DIM = D_MODEL // 2
CONV_WIDTH = 31
POOL_DIM = D_MODEL // 2
POOL_WINDOWS = (2, 4, 8, 16)
POOL_GROUP = POOL_DIM // len(POOL_WINDOWS)

GLA_HEADS = 4
GLA_DK = D_MODEL // 2
GLA_DV = D_MODEL
GLA_HK = GLA_DK // GLA_HEADS
GLA_HV = GLA_DV // GLA_HEADS
GLA_GATE_RANK = 16
GLA_GATE_NORM = 16.0
N_QKVG = 2 * GLA_DK + 2 * GLA_DV

LANES = 128
N_SLAB = CONV_DIM // LANES
assert POOL_GROUP == LANES and POOL_DIM == CONV_DIM
A_CARRY = 32
P_CARRY = 16
ROW_BLOCK = 64
FF_CHUNK = 512
N_FF = D_FF // FF_CHUNK
VMEM_LIMIT = 56 * 1024 * 1024

F32 = jnp.float32
BF16 = jnp.bfloat16


def _rms(x, g):
    return x * lax.rsqrt(jnp.mean(x * x, axis=-1, keepdims=True) + EPS) * g


def _sigmoid(x):
    return 1.0 / (1.0 + jnp.exp(-x))


def _const_spec(shape):
    zeros = (0,) * len(shape)
    return pl.BlockSpec(shape, lambda *_: zeros)


class _Layer:
    def __init__(self, array, layer):
        self.array, self.layer = array, layer
        self.shape = array.shape[1:]

    def spec(self):
        index = (self.layer,) + (0,) * len(self.shape)
        return pl.BlockSpec((None,) + self.shape, lambda *_: index, pipeline_mode=pl.Buffered(1))


def _split_consts(consts):
    arrays = [c.array if isinstance(c, _Layer) else c for c in consts]
    specs = [c.spec() if isinstance(c, _Layer) else _const_spec(c.shape) for c in consts]
    return arrays, specs


def _cast_kernel(x_ref, o_ref):
    o_ref[...] = x_ref[...].astype(o_ref.dtype)


def _to_bf16(w, block, cols=None):
    layers, rows, all_cols = w.shape
    cols = all_cols if cols is None else cols
    br, bc = block
    assert rows % br == 0 and cols % bc == 0
    spec = pl.BlockSpec((1, br, bc), lambda l, i, j: (l, i, j))
    return pl.pallas_call(
        _cast_kernel,
        grid=(layers, rows // br, cols // bc),
        in_specs=[spec],
        out_specs=spec,
        out_shape=jax.ShapeDtypeStruct((layers, rows, cols), BF16),
        name="to_bf16",
    )(w)


def _tile_maps(n_tiles, pipelined):
    if not pipelined:
        return (lambda s: (s, 0)), (lambda s: (s, 0))
    return (lambda s: (jnp.minimum(s, n_tiles - 1), 0)), (lambda s: (jnp.maximum(s - 1, 0), 0))


def _dependent_zero(v):
    bits = lax.bitcast_convert_type(v, jnp.uint32)
    bits = lax.shift_right_logical(lax.shift_right_logical(bits, jnp.uint32(16)), jnp.uint32(16))
    return lax.bitcast_convert_type(bits, F32)


def _delayed(token, hops=3):
    for _ in range(hops):
        s = jnp.sum(token, axis=-1, keepdims=True)
        token = _dependent_zero(jnp.broadcast_to(s, token.shape))
    return token


def _slabs(x):
    return jnp.stack([x[:, k * LANES:(k + 1) * LANES] for k in range(N_SLAB)])


def _unslab(x):
    return jnp.concatenate([x[k] for k in range(N_SLAB)], axis=-1)


def _conv_pool_block(r0, rb, pos0, abuf, pbuf, cbuf, mix, cw_ref, cb_ref, lng_ref, lnb_ref,
                     l_off):
    for k in range(N_SLAB):
        cols = slice(k * LANES, (k + 1) * LANES)
        acc = jnp.broadcast_to(cb_ref[:, cols], (rb, LANES))
        for j in range(CONV_WIDTH):
            off = A_CARRY - (CONV_WIDTH - 1) + j
            acc = acc + cw_ref[j:j + 1, cols] * abuf[k, pl.ds(r0 + off, rb), :]
        cbuf[k, pl.ds(r0, rb), :] = acc
    conv = [cbuf[k, pl.ds(r0, rb), :] for k in range(N_SLAB)]
    mu = sum(jnp.sum(c, axis=-1, keepdims=True) for c in conv) * (1.0 / CONV_DIM)
    cen = [c - mu for c in conv]
    var = sum(jnp.sum(c * c, axis=-1, keepdims=True) for c in cen) * (1.0 / CONV_DIM)
    rstd = lax.rsqrt(var + EPS)
    for k in range(N_SLAB):
        cols = slice(k * LANES, (k + 1) * LANES)
        y = cen[k] * rstd * lng_ref[:, cols] + lnb_ref[:, cols]
        mix[pl.ds(r0, rb), cols] = (y * _sigmoid(y)).astype(BF16)
    for k, w in enumerate(POOL_WINDOWS):
        win = pbuf[k, pl.ds(r0, P_CARRY + rb), :]
        cur = win[P_CARRY:]
        ws = win
        for step in range(w.bit_length() - 1):
            ws = ws + pltpu.roll(ws, 1 << step, axis=0)
        ws = ws[P_CARRY:]
        if l_off + 1 >= w:
            mean = ws * (1.0 / w)
        else:
            pos = pos0 + r0 + lax.broadcasted_iota(jnp.int32, (rb, LANES), 0)
            mean = ws / jnp.minimum(pos + 1, w).astype(F32)
        mix[pl.ds(r0, rb), CONV_DIM + k * LANES:CONV_DIM + (k + 1) * LANES] = (mean - cur).astype(BF16)


def _layer0_kernel(x_ref, gm_ref, win_ref, cw_ref, cb_ref, lng_ref, lnb_ref, pw_ref, ps_ref, wout_ref,
                   a0_ref, p0_ref, gf_ref, w1_ref, w2c_ref, out_ref, atail_ref, ptail_ref,
                   abuf, pbuf, cbuf, mix, hbuf, *, tile, l_off, n_tiles, tiles_per_seq, pipelined):
    s = pl.program_id(0)
    t_in_seq = lax.rem(s, tiles_per_seq)
    rb = min(ROW_BLOCK, tile)
    pos0 = l_off + t_in_seq * tile

    def conv_blocks():
        for i in range(tile // rb):
            _conv_pool_block(i * rb, rb, pos0, abuf, pbuf, cbuf, mix, cw_ref, cb_ref, lng_ref,
                             lnb_ref, l_off)

    if pipelined:
        @pl.when(s == 0)
        def _():
            mix[...] = jnp.zeros(mix.shape, mix.dtype)
            hbuf[...] = jnp.zeros(hbuf.shape, hbuf.dtype)

    @pl.when(t_in_seq == 0)
    def _():
        abuf[:, 0:A_CARRY, :] = _slabs(a0_ref[...])
        pbuf[:, 0:P_CARRY, :] = _slabs(p0_ref[...])

    u = _rms(x_ref[...], gm_ref[...]).astype(BF16)
    z = jnp.dot(u, win_ref[...], preferred_element_type=F32)
    glu = z[:, :CONV_DIM] * _sigmoid(z[:, CONV_DIM:2 * CONV_DIM])
    pin = z[:, 2 * CONV_DIM:]
    if not pipelined:
        abuf[:, A_CARRY:A_CARRY + tile, :] = _slabs(glu)
        pbuf[:, P_CARRY:P_CARRY + tile, :] = _slabs(pin)
        conv_blocks()

    pooled = jnp.dot(mix[:, CONV_DIM:], pw_ref[...], preferred_element_type=F32) * ps_ref[...]
    y = jnp.dot(mix[:, 0:CONV_DIM], wout_ref[0:CONV_DIM, :], preferred_element_type=F32)
    y = y + jnp.dot(pooled.astype(BF16), wout_ref[CONV_DIM:, :], preferred_element_type=F32)
    h1 = (hbuf[...] if pipelined else x_ref[...]) + y
    uu = _rms(h1, gf_ref[...]).astype(BF16)
    acc = h1
    if pipelined:
        token = _dependent_zero(glu[0:8, 0:LANES])
        for i in range(tile // rb):
            rows = slice(i * rb, (i + 1) * rb)
            for k in range(N_SLAB):
                zero = jnp.tile(token, (rb // 8, 1))
                cols = slice(k * LANES, (k + 1) * LANES)
                abuf[k, A_CARRY + i * rb:A_CARRY + (i + 1) * rb, :] = glu[rows, cols] + zero
                pbuf[k, P_CARRY + i * rb:P_CARRY + (i + 1) * rb, :] = pin[rows, cols] + zero
                token = _delayed(token)
    for c in range(N_FF):
        a = jnp.maximum(jnp.dot(uu, w1_ref[:, c * FF_CHUNK:(c + 1) * FF_CHUNK],
                                    preferred_element_type=F32), 0.0)
        acc = acc + jnp.dot((a * a).astype(BF16), w2c_ref[c], preferred_element_type=F32)
    out_ref[...] = acc
    if pipelined:
        conv_blocks()
        hbuf[...] = x_ref[...]

    a_tail = abuf[:, tile:tile + A_CARRY, :]
    p_tail = pbuf[:, tile:tile + P_CARRY, :]
    abuf[:, 0:A_CARRY, :] = a_tail
    pbuf[:, 0:P_CARRY, :] = p_tail

    @pl.when(s == n_tiles - 1)
    def _():
        atail_ref[...] = _unslab(a_tail)
        ptail_ref[...] = _unslab(p_tail)


def _layer0(x2d, seq, gm, win, cw, cb, lng, lnb, pw_bd, ps, wout, a0, p0, gf, w1, w2c, *, tile,
            l_off):
    n, d = x2d.shape
    assert seq % tile == 0 and n % seq == 0 and tile % min(ROW_BLOCK, tile) == 0
    n_tiles = n // tile
    pipelined = n_tiles > 1
    kern = functools.partial(_layer0_kernel, tile=tile, l_off=l_off, n_tiles=n_tiles,
                             tiles_per_seq=seq // tile, pipelined=pipelined)
    consts, const_specs = _split_consts((gm, win, cw, cb, lng, lnb, pw_bd, ps, wout, a0, p0, gf, w1,
                                         w2c))
    in_map, out_map = _tile_maps(n_tiles, pipelined)
    return pl.pallas_call(
        kern,
        grid=(n_tiles + pipelined,),
        in_specs=[pl.BlockSpec((tile, d), in_map)] + const_specs,
        out_specs=[pl.BlockSpec((tile, d), out_map),
                   _const_spec((A_CARRY, CONV_DIM)), _const_spec((P_CARRY, POOL_DIM))],
        out_shape=[jax.ShapeDtypeStruct(x2d.shape, F32),
                   jax.ShapeDtypeStruct((A_CARRY, CONV_DIM), F32),
                   jax.ShapeDtypeStruct((P_CARRY, POOL_DIM), F32)],
        scratch_shapes=[pltpu.VMEM((N_SLAB, A_CARRY + tile, LANES), F32),
                        pltpu.VMEM((N_SLAB, P_CARRY + tile, LANES), F32),
                        pltpu.VMEM((N_SLAB, tile, LANES), F32),
                        pltpu.VMEM((tile, CONV_DIM + POOL_DIM), BF16),
                        pltpu.VMEM((tile, d), F32)],
        compiler_params=pltpu.CompilerParams(dimension_semantics=("arbitrary",),
                                             vmem_limit_bytes=VMEM_LIMIT),
        name="layer0",
    )(x2d, *consts)


def _chunk_cumsum(tri_ref, x):
    tri = tri_ref[...]
    total = None
    for _ in range(3):
        part = x.astype(BF16)
        x = x - part.astype(F32)
        term = jnp.dot(tri, part, preferred_element_type=F32)
        total = term if total is None else total + term
    return total


def _gla_update(r0, z_scr, la_scr, state, tri_ref):
    rows = pl.ds(r0, CHUNK)
    cum = _chunk_cumsum(tri_ref, la_scr[rows, :])
    tot = cum[CHUNK - 1:CHUNK, :]
    kd = (z_scr[rows, GLA_DK:2 * GLA_DK] * jnp.exp(tot - cum)).astype(BF16)
    etot = jnp.exp(tot)
    for hh in range(GLA_HEADS):
        klo, vlo = hh * GLA_HK, 2 * GLA_DK + hh * GLA_HV
        v_h = z_scr[rows, vlo:vlo + GLA_HV].astype(BF16)
        upd = lax.dot_general(kd[:, klo:klo + GLA_HK], v_h, (((0,), (0,)), ((), ())),
                              preferred_element_type=F32)
        ecol = jnp.transpose(jnp.broadcast_to(etot[:, klo:klo + GLA_HK], (GLA_HK, GLA_HK)))
        state[hh] = state[hh] * jnp.tile(ecol, (1, GLA_HV // GLA_HK)) + upd


def _gla_output(r0, z_scr, og, state, hg_ref):
    rows = pl.ds(r0, CHUNK)
    q = (z_scr[rows, 0:GLA_DK] * (GLA_HK ** -0.5)).astype(BF16)
    glo = 2 * GLA_DK + GLA_DV
    for hh in range(GLA_HEADS):
        klo = hh * GLA_HK
        o = jnp.dot(q[:, klo:klo + GLA_HK], state[hh].astype(BF16),
                    preferred_element_type=F32)
        o = o * lax.rsqrt(jnp.mean(o * o, axis=-1, keepdims=True) + EPS) * hg_ref[...]
        gt = z_scr[rows, glo + hh * GLA_HV:glo + (hh + 1) * GLA_HV]
        og[rows, hh * GLA_HV:(hh + 1) * GLA_HV] = (o * (gt * _sigmoid(gt))).astype(BF16)


def _layer1_kernel(x_ref, gm_ref, wqkvg_ref, wr_ref, gw2_ref, gb_ref, hg_ref, wout_ref, s0_ref,
                   tri_ref, gf_ref, w1_ref, w2c_ref, fg_ref, out_ref, sfin_ref,
                   z_scr, la_scr, og, state, hbuf, *, tile, n_pad, n_tiles, tiles_per_seq,
                   final_norm, state_only):
    s = pl.program_id(0)
    t_in_seq = lax.rem(s, tiles_per_seq)
    n_chunks = tile // CHUNK

    if not state_only:
        @pl.when(s == 0)
        def _():
            og[...] = jnp.zeros(og.shape, og.dtype)
            hbuf[...] = jnp.zeros(hbuf.shape, hbuf.dtype)

    @pl.when(t_in_seq == 0)
    def _():
        state[...] = s0_ref[...]

    u = _rms(x_ref[...], gm_ref[...]).astype(BF16)
    z_scr[...] = jnp.dot(u, wqkvg_ref[...], preferred_element_type=F32)
    r = jnp.dot(u, wr_ref[...], preferred_element_type=F32)
    x = jnp.dot(r.astype(BF16), gw2_ref[...], preferred_element_type=F32) + gb_ref[...]
    la = (jnp.minimum(x, 0.0) - jnp.log1p(jnp.exp(-jnp.abs(x)))) * (1.0 / GLA_GATE_NORM)
    if n_pad:
        row = t_in_seq * tile + lax.broadcasted_iota(jnp.int32, (tile, GLA_DK), 0)
        la = jnp.where(row >= n_pad, la, 0.0)
    la_scr[...] = la

    if state_only:
        for c in range(n_chunks):
            _gla_update(c * CHUNK, z_scr, la_scr, state, tri_ref)
        out_ref[...] = x_ref[...]
    else:
        h1 = hbuf[...] + jnp.dot(og[...], wout_ref[...], preferred_element_type=F32)
        uu = _rms(h1, gf_ref[...]).astype(BF16)
        acc = h1
        for c in range(max(N_FF, n_chunks)):
            if c < n_chunks:
                _gla_update(c * CHUNK, z_scr, la_scr, state, tri_ref)
            if c < N_FF:
                a = jnp.maximum(jnp.dot(uu, w1_ref[:, c * FF_CHUNK:(c + 1) * FF_CHUNK],
                                    preferred_element_type=F32), 0.0)
            if c < n_chunks:
                _gla_output(c * CHUNK, z_scr, og, state, hg_ref)
            if c < N_FF:
                acc = acc + jnp.dot((a * a).astype(BF16), w2c_ref[c], preferred_element_type=F32)
        if final_norm:
            acc = _rms(acc, fg_ref[...])
        out_ref[...] = acc
        hbuf[...] = x_ref[...]

    @pl.when(s == n_tiles - 1)
    def _():
        sfin_ref[...] = state[...]


def _layer1(x2d, seq, gm, wqkvg, wr, gw2, gb, hg, wout, s0, tri, gf, w1, w2c, fg, *, tile, n_pad,
            final_norm, state_only=False):
    n, d = x2d.shape
    assert seq % tile == 0 and n % seq == 0 and tile % CHUNK == 0
    n_tiles = n // tile
    pipelined = not state_only
    kern = functools.partial(_layer1_kernel, tile=tile, n_pad=n_pad, n_tiles=n_tiles,
                             tiles_per_seq=seq // tile, final_norm=final_norm, state_only=state_only)
    consts, const_specs = _split_consts((gm, wqkvg, wr, gw2, gb, hg, wout, s0, tri, gf, w1, w2c, fg))
    state_shape = (GLA_HEADS, GLA_HK, GLA_HV)
    in_map, out_map = _tile_maps(n_tiles, pipelined)
    return pl.pallas_call(
        kern,
        grid=(n_tiles + pipelined,),
        in_specs=[pl.BlockSpec((tile, d), in_map)] + const_specs,
        out_specs=[pl.BlockSpec((tile, d), out_map), _const_spec(state_shape)],
        out_shape=[jax.ShapeDtypeStruct(x2d.shape, F32), jax.ShapeDtypeStruct(state_shape, F32)],
        scratch_shapes=[pltpu.VMEM((tile, N_QKVG), F32),
                        pltpu.VMEM((tile, GLA_DK), F32),
                        pltpu.VMEM((tile, GLA_DV), BF16),
                        pltpu.VMEM(state_shape, F32),
                        pltpu.VMEM((tile, d), F32)],
        compiler_params=pltpu.CompilerParams(dimension_semantics=("arbitrary",),
                                             vmem_limit_bytes=VMEM_LIMIT),
        name="layer1",
    )(x2d, *consts)


def kernel(x, meta_tokens, mix_norm_g, ffn_norm_g, ffn_w1, ffn_w2, cp_w_in, cp_conv_w, cp_conv_b,
           cp_ln_g, cp_ln_b, cp_pool_w, cp_pool_scale, cp_w_out, gla_w_in, gla_gate_w2, gla_gate_b,
           gla_head_g, gla_w_out, final_norm_g):
    bsz, seq, d = x.shape
    row = lambda v: v.reshape(1, -1).astype(F32)

    depth = ffn_w1.shape[0]
    w1 = _to_bf16(ffn_w1, (d, 2 * FF_CHUNK))
    w2c = _to_bf16(ffn_w2, (2 * FF_CHUNK, d)).reshape(depth, N_FF, FF_CHUNK, d)
    cp_win = _Layer(_to_bf16(cp_w_in, (d, FF_CHUNK)), 0)
    cp_wout = _Layer(_to_bf16(cp_w_out, (d, FF_CHUNK)), 0)
    pw_bd = jnp.zeros((POOL_DIM, POOL_DIM), F32)
    for gi in range(len(POOL_WINDOWS)):
        lo = gi * POOL_GROUP
        pw_bd = lax.dynamic_update_slice(pw_bd, cp_pool_w[0, gi], (lo, lo))
    pw_bd = pw_bd.astype(BF16)
    wqkvg = _Layer(_to_bf16(gla_w_in, (d, FF_CHUNK), cols=N_QKVG), 0)
    wr = jnp.pad(gla_w_in[0, :, N_QKVG:], ((0, 0), (0, LANES - GLA_GATE_RANK))).astype(BF16)
    gw2 = jnp.pad(gla_gate_w2[0], ((0, LANES - GLA_GATE_RANK), (0, 0))).astype(BF16)
    gla_wout = _Layer(_to_bf16(gla_w_out, (d, FF_CHUNK)), 0)
    tri = jnp.tril(jnp.ones((CHUNK, CHUNK), BF16))
    fg = row(final_norm_g)

    l0_args = (row(mix_norm_g[0]), cp_win, cp_conv_w[0].astype(F32), row(cp_conv_b[0]),
               row(cp_ln_g[0]), row(cp_ln_b[0]), pw_bd, row(cp_pool_scale[0]), cp_wout)
    l0_mlp = (row(ffn_norm_g[0]), _Layer(w1, 0), _Layer(w2c, 0))
    l1_args = (row(mix_norm_g[1]), wqkvg, wr, gw2, row(gla_gate_b[0]), row(gla_head_g[0]), gla_wout)
    l1_mlp = (row(ffn_norm_g[1]), _Layer(w1, 1), _Layer(w2c, 1), fg)

    hm, a_tail, p_tail = _layer0(meta_tokens.astype(F32), N_META, *l0_args,
                                 jnp.zeros((A_CARRY, CONV_DIM), F32),
                                 jnp.zeros((P_CARRY, POOL_DIM), F32), *l0_mlp, tile=N_META, l_off=0)
    hm = jnp.pad(hm, ((CHUNK - N_META, 0), (0, 0)))
    s_zero = jnp.zeros((GLA_HEADS, GLA_HK, GLA_HV), F32)
    _, s_meta = _layer1(hm, CHUNK, *l1_args, s_zero, tri, *l1_mlp, tile=CHUNK,
                        n_pad=CHUNK - N_META, final_norm=False, state_only=True)

    tile = 512
    h, _, _ = _layer0(x.astype(F32).reshape(bsz * seq, d), seq, *l0_args, a_tail, p_tail, *l0_mlp,
                      tile=tile, l_off=N_META)
    out, _ = _layer1(h, seq, *l1_args, s_meta, tri, *l1_mlp, tile=tile, n_pad=0, final_norm=True)
    return out.reshape(bsz, seq, d).astype(x.dtype)
```

```python
import functools

import jax
import jax.numpy as jnp
from jax import lax
from jax.experimental import pallas as pl
from jax.experimental.pallas import tpu as pltpu

D_MODEL = 1024
N_META = 16
CHUNK = 64
D_FF = 4 * D_MODEL
EPS = 1e-5

CONV_DIM = D_MODEL // 2
CONV_WIDTH = 31
POOL_DIM = D_MODEL // 2
POOL_WINDOWS = (2, 4, 8, 16)
POOL_GROUP = POOL_DIM // len(POOL_WINDOWS)

GLA_HEADS = 4
GLA_DK = D_MODEL // 2
GLA_DV = D_MODEL
GLA_HK = GLA_DK // GLA_HEADS
GLA_HV = GLA_DV // GLA_HEADS
GLA_GATE_RANK = 16
GLA_GATE_NORM = 16.0
N_QKVG = 2 * GLA_DK + 2 * GLA_DV

LANES = 128
N_SLAB = CONV_DIM // LANES
assert POOL_GROUP == LANES and POOL_DIM == CONV_DIM
A_CARRY = 32
P_CARRY = 16
ROW_BLOCK = 64
FF_CHUNK = 512
N_FF = D_FF // FF_CHUNK
VMEM_LIMIT = 56 * 1024 * 1024

F32 = jnp.float32
BF16 = jnp.bfloat16


def _rms(x, g):
    return x * lax.rsqrt(jnp.mean(x * x, axis=-1, keepdims=True) + EPS) * g


def _sigmoid(x):
    return 1.0 / (1.0 + jnp.exp(-x))


def _const_spec(shape):
    zeros = (0,) * len(shape)
    return pl.BlockSpec(shape, lambda *_: zeros)


class _Layer:
    def __init__(self, array, layer):
        self.array, self.layer = array, layer
        self.shape = array.shape[1:]

    def spec(self):
        index = (self.layer,) + (0,) * len(self.shape)
        return pl.BlockSpec((None,) + self.shape, lambda *_: index, pipeline_mode=pl.Buffered(1))


def _split_consts(consts):
    arrays = [c.array if isinstance(c, _Layer) else c for c in consts]
    specs = [c.spec() if isinstance(c, _Layer) else _const_spec(c.shape) for c in consts]
    return arrays, specs


def _cast_kernel(x_ref, o_ref):
    o_ref[...] = x_ref[...].astype(o_ref.dtype)


def _to_bf16(w, block, cols=None):
    layers, rows, all_cols = w.shape
    cols = all_cols if cols is None else cols
    br, bc = block
    assert rows % br == 0 and cols % bc == 0
    spec = pl.BlockSpec((1, br, bc), lambda l, i, j: (l, i, j))
    return pl.pallas_call(
        _cast_kernel,
        grid=(layers, rows // br, cols // bc),
        in_specs=[spec],
        out_specs=spec,
        out_shape=jax.ShapeDtypeStruct((layers, rows, cols), BF16),
        name="to_bf16",
    )(w)


def _tile_maps(n_tiles, pipelined):
    if not pipelined:
        return (lambda s: (s, 0)), (lambda s: (s, 0))
    return (lambda s: (jnp.minimum(s, n_tiles - 1), 0)), (lambda s: (jnp.maximum(s - 1, 0), 0))


def _dependent_zero(v):
    bits = lax.bitcast_convert_type(v, jnp.uint32)
    bits = lax.shift_right_logical(lax.shift_right_logical(bits, jnp.uint32(16)), jnp.uint32(16))
    return lax.bitcast_convert_type(bits, F32)


def _delayed(token, hops=3):
    for _ in range(hops):
        s = jnp.sum(token, axis=-1, keepdims=True)
        token = _dependent_zero(jnp.broadcast_to(s, token.shape))
    return token


def _slabs(x):
    return jnp.stack([x[:, k * LANES:(k + 1) * LANES] for k in range(N_SLAB)])


def _unslab(x):
    return jnp.concatenate([x[k] for k in range(N_SLAB)], axis=-1)


def _conv_pool_block(r0, rb, pos0, abuf, pbuf, cbuf, mix, cw_ref, cb_ref, lng_ref, lnb_ref,
                     l_off):
    for k in range(N_SLAB):
        cols = slice(k * LANES, (k + 1) * LANES)
        acc = jnp.broadcast_to(cb_ref[:, cols], (rb, LANES))
        for j in range(CONV_WIDTH):
            off = A_CARRY - (CONV_WIDTH - 1) + j
            acc = acc + cw_ref[j:j + 1, cols] * abuf[k, pl.ds(r0 + off, rb), :]
        cbuf[k, pl.ds(r0, rb), :] = acc
    conv = [cbuf[k, pl.ds(r0, rb), :] for k in range(N_SLAB)]
    mu = sum(jnp.sum(c, axis=-1, keepdims=True) for c in conv) * (1.0 / CONV_DIM)
    cen = [c - mu for c in conv]
    var = sum(jnp.sum(c * c, axis=-1, keepdims=True) for c in cen) * (1.0 / CONV_DIM)
    rstd = lax.rsqrt(var + EPS)
    for k in range(N_SLAB):
        cols = slice(k * LANES, (k + 1) * LANES)
        y = cen[k] * rstd * lng_ref[:, cols] + lnb_ref[:, cols]
        mix[pl.ds(r0, rb), cols] = (y * _sigmoid(y)).astype(BF16)
    for k, w in enumerate(POOL_WINDOWS):
        win = pbuf[k, pl.ds(r0, P_CARRY + rb), :]
        cur = win[P_CARRY:]
        ws = win
        for step in range(w.bit_length() - 1):
            ws = ws + pltpu.roll(ws, 1 << step, axis=0)
        ws = ws[P_CARRY:]
        if l_off + 1 >= w:
            mean = ws * (1.0 / w)
        else:
            pos = pos0 + r0 + lax.broadcasted_iota(jnp.int32, (rb, LANES), 0)
            mean = ws / jnp.minimum(pos + 1, w).astype(F32)
        mix[pl.ds(r0, rb), CONV_DIM + k * LANES:CONV_DIM + (k + 1) * LANES] = (mean - cur).astype(BF16)


def _layer0_kernel(x_ref, gm_ref, win_ref, cw_ref, cb_ref, lng_ref, lnb_ref, pw_ref, ps_ref, wout_ref,
                   a0_ref, p0_ref, gf_ref, w1_ref, w2c_ref, out_ref, atail_ref, ptail_ref,
                   abuf, pbuf, cbuf, mix, hbuf, *, tile, l_off, n_tiles, tiles_per_seq, pipelined):
    s = pl.program_id(0)
    t_in_seq = lax.rem(s, tiles_per_seq)
    rb = min(ROW_BLOCK, tile)
    pos0 = l_off + t_in_seq * tile

    def conv_blocks():
        for i in range(tile // rb):
            _conv_pool_block(i * rb, rb, pos0, abuf, pbuf, cbuf, mix, cw_ref, cb_ref, lng_ref,
                             lnb_ref, l_off)

    @pl.when(t_in_seq == 0)
    def _():
        abuf[:, 0:A_CARRY, :] = _slabs(a0_ref[...])
        pbuf[:, 0:P_CARRY, :] = _slabs(p0_ref[...])

    def input_projection():
        u = _rms(x_ref[...], gm_ref[...]).astype(BF16)
        z = jnp.dot(u, win_ref[...], preferred_element_type=F32)
        return z[:, :CONV_DIM] * _sigmoid(z[:, CONV_DIM:2 * CONV_DIM]), z[:, 2 * CONV_DIM:]

    def output_projection():
        pooled = jnp.dot(mix[:, CONV_DIM:], pw_ref[...], preferred_element_type=F32) * ps_ref[...]
        y = jnp.dot(mix[:, 0:CONV_DIM], wout_ref[0:CONV_DIM, :], preferred_element_type=F32)
        return y + jnp.dot(pooled.astype(BF16), wout_ref[CONV_DIM:, :], preferred_element_type=F32)

    def mlp(h1):
        uu = _rms(h1, gf_ref[...]).astype(BF16)
        acc = h1
        for c in range(N_FF):
            a = jnp.maximum(jnp.dot(uu, w1_ref[:, c * FF_CHUNK:(c + 1) * FF_CHUNK],
                                    preferred_element_type=F32), 0.0)
            acc = acc + jnp.dot((a * a).astype(BF16), w2c_ref[c], preferred_element_type=F32)
        out_ref[...] = acc

    def finish_front():
        a_tail = abuf[:, tile:tile + A_CARRY, :]
        p_tail = pbuf[:, tile:tile + P_CARRY, :]
        abuf[:, 0:A_CARRY, :] = a_tail
        pbuf[:, 0:P_CARRY, :] = p_tail

        @pl.when(s == n_tiles - 1)
        def _():
            atail_ref[...] = _unslab(a_tail)
            ptail_ref[...] = _unslab(p_tail)

    def front_only():
        glu, pin = input_projection()
        abuf[:, A_CARRY:A_CARRY + tile, :] = _slabs(glu)
        pbuf[:, P_CARRY:P_CARRY + tile, :] = _slabs(pin)
        conv_blocks()

    def front_and_back():
        y = output_projection()
        glu, pin = input_projection()
        h1 = hbuf[...] + y
        token = _dependent_zero(glu[0:8, 0:LANES])
        for i in range(tile // rb):
            rows = slice(i * rb, (i + 1) * rb)
            for k in range(N_SLAB):
                zero = jnp.tile(token, (rb // 8, 1))
                cols = slice(k * LANES, (k + 1) * LANES)
                abuf[k, A_CARRY + i * rb:A_CARRY + (i + 1) * rb, :] = glu[rows, cols] + zero
                pbuf[k, P_CARRY + i * rb:P_CARRY + (i + 1) * rb, :] = pin[rows, cols] + zero
                token = _delayed(token)
        mlp(h1)
        conv_blocks()

    if not pipelined:
        front_only()
        mlp(x_ref[...] + output_projection())
        finish_front()
    else:
        @pl.when(s == 0)
        def _():
            front_only()
            hbuf[...] = x_ref[...]
            finish_front()

        @pl.when(jnp.logical_and(s > 0, s < n_tiles))
        def _():
            front_and_back()
            hbuf[...] = x_ref[...]
            finish_front()

        @pl.when(s == n_tiles)
        def _():
            mlp(hbuf[...] + output_projection())


def _layer0(x2d, seq, gm, win, cw, cb, lng, lnb, pw_bd, ps, wout, a0, p0, gf, w1, w2c, *, tile,
            l_off):
    n, d = x2d.shape
    assert seq % tile == 0 and n % seq == 0 and tile % min(ROW_BLOCK, tile) == 0
    n_tiles = n // tile
    pipelined = n_tiles > 1
    kern = functools.partial(_layer0_kernel, tile=tile, l_off=l_off, n_tiles=n_tiles,
                             tiles_per_seq=seq // tile, pipelined=pipelined)
    consts, const_specs = _split_consts((gm, win, cw, cb, lng, lnb, pw_bd, ps, wout, a0, p0, gf, w1,
                                         w2c))
    in_map, out_map = _tile_maps(n_tiles, pipelined)
    return pl.pallas_call(
        kern,
        grid=(n_tiles + pipelined,),
        in_specs=[pl.BlockSpec((tile, d), in_map)] + const_specs,
        out_specs=[pl.BlockSpec((tile, d), out_map),
                   _const_spec((A_CARRY, CONV_DIM)), _const_spec((P_CARRY, POOL_DIM))],
        out_shape=[jax.ShapeDtypeStruct(x2d.shape, F32),
                   jax.ShapeDtypeStruct((A_CARRY, CONV_DIM), F32),
                   jax.ShapeDtypeStruct((P_CARRY, POOL_DIM), F32)],
        scratch_shapes=[pltpu.VMEM((N_SLAB, A_CARRY + tile, LANES), F32),
                        pltpu.VMEM((N_SLAB, P_CARRY + tile, LANES), F32),
                        pltpu.VMEM((N_SLAB, tile, LANES), F32),
                        pltpu.VMEM((tile, CONV_DIM + POOL_DIM), BF16),
                        pltpu.VMEM((tile, d), F32)],
        compiler_params=pltpu.CompilerParams(dimension_semantics=("arbitrary",),
                                             vmem_limit_bytes=VMEM_LIMIT),
        name="layer0",
    )(x2d, *consts)


def _chunk_cumsum(tri_ref, x):
    tri = tri_ref[...]
    total = None
    for _ in range(3):
        part = x.astype(BF16)
        x = x - part.astype(F32)
        term = jnp.dot(tri, part, preferred_element_type=F32)
        total = term if total is None else total + term
    return total


def _gla_update(r0, z_scr, la_scr, state, tri_ref):
    rows = pl.ds(r0, CHUNK)
    cum = _chunk_cumsum(tri_ref, la_scr[rows, :])
    tot = cum[CHUNK - 1:CHUNK, :]
    kd = (z_scr[rows, GLA_DK:2 * GLA_DK] * jnp.exp(tot - cum)).astype(BF16)
    etot = jnp.exp(tot)
    for hh in range(GLA_HEADS):
        klo, vlo = hh * GLA_HK, 2 * GLA_DK + hh * GLA_HV
        v_h = z_scr[rows, vlo:vlo + GLA_HV].astype(BF16)
        upd = lax.dot_general(kd[:, klo:klo + GLA_HK], v_h, (((0,), (0,)), ((), ())),
                              preferred_element_type=F32)
        ecol = jnp.transpose(jnp.broadcast_to(etot[:, klo:klo + GLA_HK], (GLA_HK, GLA_HK)))
        state[hh] = state[hh] * jnp.tile(ecol, (1, GLA_HV // GLA_HK)) + upd


def _gla_output(r0, z_scr, og, state, hg_ref):
    rows = pl.ds(r0, CHUNK)
    q = (z_scr[rows, 0:GLA_DK] * (GLA_HK ** -0.5)).astype(BF16)
    glo = 2 * GLA_DK + GLA_DV
    for hh in range(GLA_HEADS):
        klo = hh * GLA_HK
        o = jnp.dot(q[:, klo:klo + GLA_HK], state[hh].astype(BF16),
                    preferred_element_type=F32)
        o = o * lax.rsqrt(jnp.mean(o * o, axis=-1, keepdims=True) + EPS) * hg_ref[...]
        gt = z_scr[rows, glo + hh * GLA_HV:glo + (hh + 1) * GLA_HV]
        og[rows, hh * GLA_HV:(hh + 1) * GLA_HV] = (o * (gt * _sigmoid(gt))).astype(BF16)


def _layer1_kernel(x_ref, gm_ref, wqkvg_ref, wr_ref, gw2_ref, gb_ref, hg_ref, wout_ref, s0_ref,
                   tri_ref, gf_ref, w1_ref, w2c_ref, fg_ref, out_ref, sfin_ref,
                   z_scr, la_scr, og, state, hbuf, *, tile, n_pad, n_tiles, tiles_per_seq,
                   final_norm, state_only):
    s = pl.program_id(0)
    t_in_seq = lax.rem(s, tiles_per_seq)
    n_chunks = tile // CHUNK

    @pl.when(t_in_seq == 0)
    def _():
        state[...] = s0_ref[...]

    def input_projection():
        u = _rms(x_ref[...], gm_ref[...]).astype(BF16)
        r = jnp.dot(u, wr_ref[...], preferred_element_type=F32)
        x = jnp.dot(r.astype(BF16), gw2_ref[...], preferred_element_type=F32) + gb_ref[...]
        la = (jnp.minimum(x, 0.0) - jnp.log1p(jnp.exp(-jnp.abs(x)))) * (1.0 / GLA_GATE_NORM)
        if n_pad:
            row = t_in_seq * tile + lax.broadcasted_iota(jnp.int32, (tile, GLA_DK), 0)
            la = jnp.where(row >= n_pad, la, 0.0)
        la_scr[...] = la
        z_scr[...] = jnp.dot(u, wqkvg_ref[...], preferred_element_type=F32)

    def output_projection():
        return hbuf[...] + jnp.dot(og[...], wout_ref[...], preferred_element_type=F32)

    def mlp_and_gla(h1, with_gla):
        uu = _rms(h1, gf_ref[...]).astype(BF16)
        acc = h1
        gla_chunks = n_chunks if with_gla else 0
        for c in range(max(N_FF, gla_chunks)):
            if c < gla_chunks:
                _gla_update(c * CHUNK, z_scr, la_scr, state, tri_ref)
            if c < N_FF:
                a = jnp.maximum(jnp.dot(uu, w1_ref[:, c * FF_CHUNK:(c + 1) * FF_CHUNK],
                                        preferred_element_type=F32), 0.0)
            if c < gla_chunks:
                _gla_output(c * CHUNK, z_scr, og, state, hg_ref)
            if c < N_FF:
                acc = acc + jnp.dot((a * a).astype(BF16), w2c_ref[c], preferred_element_type=F32)
        if final_norm:
            acc = _rms(acc, fg_ref[...])
        out_ref[...] = acc

    if state_only:
        input_projection()
        for c in range(n_chunks):
            _gla_update(c * CHUNK, z_scr, la_scr, state, tri_ref)
        out_ref[...] = x_ref[...]
    else:
        @pl.when(s == 0)
        def _():
            input_projection()
            for c in range(n_chunks):
                _gla_update(c * CHUNK, z_scr, la_scr, state, tri_ref)
                _gla_output(c * CHUNK, z_scr, og, state, hg_ref)
            hbuf[...] = x_ref[...]

        @pl.when(jnp.logical_and(s > 0, s < n_tiles))
        def _():
            h1 = output_projection()
            input_projection()
            mlp_and_gla(h1, with_gla=True)
            hbuf[...] = x_ref[...]

        @pl.when(s == n_tiles)
        def _():
            mlp_and_gla(output_projection(), with_gla=False)

    @pl.when(s == n_tiles - 1)
    def _():
        sfin_ref[...] = state[...]


def _layer1(x2d, seq, gm, wqkvg, wr, gw2, gb, hg, wout, s0, tri, gf, w1, w2c, fg, *, tile, n_pad,
            final_norm, state_only=False):
    n, d = x2d.shape
    assert seq % tile == 0 and n % seq == 0 and tile % CHUNK == 0
    n_tiles = n // tile
    pipelined = not state_only
    if state_only:
        wout = w1 = w2c = jnp.zeros((8, LANES), BF16)
    kern = functools.partial(_layer1_kernel, tile=tile, n_pad=n_pad, n_tiles=n_tiles,
                             tiles_per_seq=seq // tile, final_norm=final_norm, state_only=state_only)
    consts, const_specs = _split_consts((gm, wqkvg, wr, gw2, gb, hg, wout, s0, tri, gf, w1, w2c, fg))
    state_shape = (GLA_HEADS, GLA_HK, GLA_HV)
    in_map, out_map = _tile_maps(n_tiles, pipelined)
    return pl.pallas_call(
        kern,
        grid=(n_tiles + pipelined,),
        in_specs=[pl.BlockSpec((tile, d), in_map)] + const_specs,
        out_specs=[pl.BlockSpec((tile, d), out_map), _const_spec(state_shape)],
        out_shape=[jax.ShapeDtypeStruct(x2d.shape, F32), jax.ShapeDtypeStruct(state_shape, F32)],
        scratch_shapes=[pltpu.VMEM((tile, N_QKVG), F32),
                        pltpu.VMEM((tile, GLA_DK), F32),
                        pltpu.VMEM((tile, GLA_DV), BF16),
                        pltpu.VMEM(state_shape, F32),
                        pltpu.VMEM((tile, d), F32)],
        compiler_params=pltpu.CompilerParams(dimension_semantics=("arbitrary",),
                                             vmem_limit_bytes=VMEM_LIMIT),
        name="layer1",
    )(x2d, *consts)


def kernel(x, meta_tokens, mix_norm_g, ffn_norm_g, ffn_w1, ffn_w2, cp_w_in, cp_conv_w, cp_conv_b,
           cp_ln_g, cp_ln_b, cp_pool_w, cp_pool_scale, cp_w_out, gla_w_in, gla_gate_w2, gla_gate_b,
           gla_head_g, gla_w_out, final_norm_g):
    bsz, seq, d = x.shape
    row = lambda v: v.reshape(1, -1).astype(F32)

    depth = ffn_w1.shape[0]
    w1 = _to_bf16(ffn_w1, (d, 2 * FF_CHUNK))
    w2c = _to_bf16(ffn_w2, (2 * FF_CHUNK, d)).reshape(depth, N_FF, FF_CHUNK, d)
    cp_win = _Layer(_to_bf16(cp_w_in, (d, FF_CHUNK)), 0)
    cp_wout = _Layer(_to_bf16(cp_w_out, (d, FF_CHUNK)), 0)
    pw_bd = jnp.zeros((POOL_DIM, POOL_DIM), F32)
    for gi in range(len(POOL_WINDOWS)):
        lo = gi * POOL_GROUP
        pw_bd = lax.dynamic_update_slice(pw_bd, cp_pool_w[0, gi], (lo, lo))
    pw_bd = pw_bd.astype(BF16)
    wqkvg = _Layer(_to_bf16(gla_w_in, (d, FF_CHUNK), cols=N_QKVG), 0)
    wr = jnp.pad(gla_w_in[0, :, N_QKVG:], ((0, 0), (0, LANES - GLA_GATE_RANK))).astype(BF16)
    gw2 = jnp.pad(gla_gate_w2[0], ((0, LANES - GLA_GATE_RANK), (0, 0))).astype(BF16)
    gla_wout = _Layer(_to_bf16(gla_w_out, (d, FF_CHUNK)), 0)
    tri = jnp.tril(jnp.ones((CHUNK, CHUNK), BF16))
    fg = row(final_norm_g)

    l0_args = (row(mix_norm_g[0]), cp_win, cp_conv_w[0].astype(F32), row(cp_conv_b[0]),
               row(cp_ln_g[0]), row(cp_ln_b[0]), pw_bd, row(cp_pool_scale[0]), cp_wout)
    l0_mlp = (row(ffn_norm_g[0]), _Layer(w1, 0), _Layer(w2c, 0))
    l1_args = (row(mix_norm_g[1]), wqkvg, wr, gw2, row(gla_gate_b[0]), row(gla_head_g[0]), gla_wout)
    l1_mlp = (row(ffn_norm_g[1]), _Layer(w1, 1), _Layer(w2c, 1), fg)

    hm, a_tail, p_tail = _layer0(meta_tokens.astype(F32), N_META, *l0_args,
                                 jnp.zeros((A_CARRY, CONV_DIM), F32),
                                 jnp.zeros((P_CARRY, POOL_DIM), F32), *l0_mlp, tile=N_META, l_off=0)
    hm = jnp.pad(hm, ((CHUNK - N_META, 0), (0, 0)))
    s_zero = jnp.zeros((GLA_HEADS, GLA_HK, GLA_HV), F32)
    _, s_meta = _layer1(hm, CHUNK, *l1_args, s_zero, tri, *l1_mlp, tile=CHUNK,
                        n_pad=CHUNK - N_META, final_norm=False, state_only=True)

    tile = 512
    h, _, _ = _layer0(x.astype(F32).reshape(bsz * seq, d), seq, *l0_args, a_tail, p_tail, *l0_mlp,
                      tile=tile, l_off=N_META)
    out, _ = _layer1(h, seq, *l1_args, s_meta, tri, *l1_mlp, tile=tile, n_pad=0, final_norm=True)
    return out.reshape(bsz, seq, d).astype(x.dtype)
```

```python
import functools

import jax
import jax.numpy as jnp
from jax import lax
from jax.experimental import pallas as pl
from jax.experimental.pallas import tpu as pltpu

D_MODEL = 1024
N_META = 16
CHUNK = 64
D_FF = 4 * D_MODEL
EPS = 1e-5

CONV_DIM = D_MODEL // 2
CONV_WIDTH = 31
POOL_DIM = D_MODEL // 2
POOL_WINDOWS = (2, 4, 8, 16)
POOL_GROUP = POOL_DIM // len(POOL_WINDOWS)

GLA_HEADS = 4
GLA_DK = D_MODEL // 2
GLA_DV = D_MODEL
GLA_HK = GLA_DK // GLA_HEADS
GLA_HV = GLA_DV // GLA_HEADS
GLA_GATE_RANK = 16
GLA_GATE_NORM = 16.0
N_QKVG = 2 * GLA_DK + 2 * GLA_DV

LANES = 128
N_SLAB = CONV_DIM // LANES
assert POOL_GROUP == LANES and POOL_DIM == CONV_DIM
A_CARRY = 32
P_CARRY = 16
ROW_BLOCK = 64
FF_CHUNK = 512
N_FF = D_FF // FF_CHUNK
VMEM_LIMIT = 56 * 1024 * 1024

F32 = jnp.float32
BF16 = jnp.bfloat16


def _rms(x, g):
    return x * lax.rsqrt(jnp.mean(x * x, axis=-1, keepdims=True) + EPS) * g


def _sigmoid(x):
    return 1.0 / (1.0 + jnp.exp(-x))


def _const_spec(shape):
    zeros = (0,) * len(shape)
    return pl.BlockSpec(shape, lambda *_: zeros)


class _Layer:
    def __init__(self, array, layer):
        self.array, self.layer = array, layer
        self.shape = array.shape[1:]

    def spec(self):
        index = (self.layer,) + (0,) * len(self.shape)
        return pl.BlockSpec((None,) + self.shape, lambda *_: index, pipeline_mode=pl.Buffered(1))


def _split_consts(consts):
    arrays = [c.array if isinstance(c, _Layer) else c for c in consts]
    specs = [c.spec() if isinstance(c, _Layer) else _const_spec(c.shape) for c in consts]
    return arrays, specs


def _cast_kernel(x_ref, o_ref):
    o_ref[...] = x_ref[...].astype(o_ref.dtype)


def _to_bf16(w, block, cols=None):
    layers, rows, all_cols = w.shape
    cols = all_cols if cols is None else cols
    br, bc = block
    assert rows % br == 0 and cols % bc == 0
    spec = pl.BlockSpec((1, br, bc), lambda l, i, j: (l, i, j))
    return pl.pallas_call(
        _cast_kernel,
        grid=(layers, rows // br, cols // bc),
        in_specs=[spec],
        out_specs=spec,
        out_shape=jax.ShapeDtypeStruct((layers, rows, cols), BF16),
        name="to_bf16",
    )(w)


def _tile_maps(n_tiles, pipelined):
    if not pipelined:
        return (lambda s: (s, 0)), (lambda s: (s, 0))
    return (lambda s: (jnp.minimum(s, n_tiles - 1), 0)), (lambda s: (jnp.maximum(s - 1, 0), 0))


def _dependent_zero(v):
    bits = lax.bitcast_convert_type(v, jnp.uint32)
    bits = lax.shift_right_logical(lax.shift_right_logical(bits, jnp.uint32(16)), jnp.uint32(16))
    return lax.bitcast_convert_type(bits, F32)


def _delayed(token, hops=3):
    for _ in range(hops):
        s = jnp.sum(token, axis=-1, keepdims=True)
        token = _dependent_zero(jnp.broadcast_to(s, token.shape))
    return token


def _slabs(x):
    return jnp.stack([x[:, k * LANES:(k + 1) * LANES] for k in range(N_SLAB)])


def _unslab(x):
    return jnp.concatenate([x[k] for k in range(N_SLAB)], axis=-1)


def _conv_pool_block(r0, rb, pos0, abuf, pbuf, cbuf, mix, cw_ref, cb_ref, lng_ref, lnb_ref,
                     l_off):
    for k in range(N_SLAB):
        cols = slice(k * LANES, (k + 1) * LANES)
        acc = jnp.broadcast_to(cb_ref[:, cols], (rb, LANES))
        for j in range(CONV_WIDTH):
            off = A_CARRY - (CONV_WIDTH - 1) + j
            acc = acc + cw_ref[j:j + 1, cols] * abuf[k, pl.ds(r0 + off, rb), :]
        cbuf[k, pl.ds(r0, rb), :] = acc
    conv = [cbuf[k, pl.ds(r0, rb), :] for k in range(N_SLAB)]
    mu = sum(jnp.sum(c, axis=-1, keepdims=True) for c in conv) * (1.0 / CONV_DIM)
    cen = [c - mu for c in conv]
    var = sum(jnp.sum(c * c, axis=-1, keepdims=True) for c in cen) * (1.0 / CONV_DIM)
    rstd = lax.rsqrt(var + EPS)
    for k in range(N_SLAB):
        cols = slice(k * LANES, (k + 1) * LANES)
        y = cen[k] * rstd * lng_ref[:, cols] + lnb_ref[:, cols]
        mix[pl.ds(r0, rb), cols] = (y * _sigmoid(y)).astype(BF16)
    for k, w in enumerate(POOL_WINDOWS):
        win = pbuf[k, pl.ds(r0, P_CARRY + rb), :]
        cur = win[P_CARRY:]
        ws = win
        for step in range(w.bit_length() - 1):
            ws = ws + pltpu.roll(ws, 1 << step, axis=0)
        ws = ws[P_CARRY:]
        if l_off + 1 >= w:
            mean = ws * (1.0 / w)
        else:
            pos = pos0 + r0 + lax.broadcasted_iota(jnp.int32, (rb, LANES), 0)
            mean = ws / jnp.minimum(pos + 1, w).astype(F32)
        mix[pl.ds(r0, rb), CONV_DIM + k * LANES:CONV_DIM + (k + 1) * LANES] = (mean - cur).astype(BF16)


def _layer0_kernel(x_ref, gm_ref, win_ref, cw_ref, cb_ref, lng_ref, lnb_ref, pw_ref, ps_ref, wout_ref,
                   a0_ref, p0_ref, gf_ref, w1_ref, w2c_ref, out_ref, atail_ref, ptail_ref,
                   abuf, pbuf, cbuf, mix, hbuf, *, tile, l_off, n_tiles, tiles_per_seq, pipelined):
    s = pl.program_id(0)
    t_in_seq = lax.rem(s, tiles_per_seq)
    rb = min(ROW_BLOCK, tile)
    pos0 = l_off + t_in_seq * tile

    def conv_blocks():
        for i in range(tile // rb):
            _conv_pool_block(i * rb, rb, pos0, abuf, pbuf, cbuf, mix, cw_ref, cb_ref, lng_ref,
                             lnb_ref, l_off)

    @pl.when(t_in_seq == 0)
    def _():
        abuf[:, 0:A_CARRY, :] = _slabs(a0_ref[...])
        pbuf[:, 0:P_CARRY, :] = _slabs(p0_ref[...])

    def input_projection():
        u = _rms(x_ref[...], gm_ref[...]).astype(BF16)
        z = jnp.dot(u, win_ref[...], preferred_element_type=F32)
        return z[:, :CONV_DIM] * _sigmoid(z[:, CONV_DIM:2 * CONV_DIM]), z[:, 2 * CONV_DIM:]

    def output_projection():
        pooled = jnp.dot(mix[:, CONV_DIM:], pw_ref[...], preferred_element_type=F32) * ps_ref[...]
        y = jnp.dot(mix[:, 0:CONV_DIM], wout_ref[0:CONV_DIM, :], preferred_element_type=F32)
        return y + jnp.dot(pooled.astype(BF16), wout_ref[CONV_DIM:, :], preferred_element_type=F32)

    def mlp(h1):
        uu = _rms(h1, gf_ref[...]).astype(BF16)
        acc = h1
        for c in range(N_FF):
            a = jnp.maximum(jnp.dot(uu, w1_ref[:, c * FF_CHUNK:(c + 1) * FF_CHUNK],
                                    preferred_element_type=F32), 0.0)
            acc = acc + jnp.dot((a * a).astype(BF16), w2c_ref[c], preferred_element_type=F32)
        out_ref[...] = acc

    def finish_front():
        a_tail = abuf[:, tile:tile + A_CARRY, :]
        p_tail = pbuf[:, tile:tile + P_CARRY, :]
        abuf[:, 0:A_CARRY, :] = a_tail
        pbuf[:, 0:P_CARRY, :] = p_tail

        @pl.when(s == n_tiles - 1)
        def _():
            atail_ref[...] = _unslab(a_tail)
            ptail_ref[...] = _unslab(p_tail)

    def front_only():
        glu, pin = input_projection()
        abuf[:, A_CARRY:A_CARRY + tile, :] = _slabs(glu)
        pbuf[:, P_CARRY:P_CARRY + tile, :] = _slabs(pin)
        conv_blocks()

    def front_and_back():
        y = output_projection()
        glu, pin = input_projection()
        h1 = hbuf[...] + y
        token = _dependent_zero(glu[0:8, 0:LANES])
        for i in range(tile // rb):
            rows = slice(i * rb, (i + 1) * rb)
            for k in range(N_SLAB):
                zero = jnp.tile(token, (rb // 8, 1))
                cols = slice(k * LANES, (k + 1) * LANES)
                abuf[k, A_CARRY + i * rb:A_CARRY + (i + 1) * rb, :] = glu[rows, cols] + zero
                pbuf[k, P_CARRY + i * rb:P_CARRY + (i + 1) * rb, :] = pin[rows, cols] + zero
                token = _delayed(token)
        mlp(h1)
        conv_blocks()

    if not pipelined:
        front_only()
        mlp(x_ref[...] + output_projection())
        finish_front()
    else:
        @pl.when(s == 0)
        def _():
            front_only()
            hbuf[...] = x_ref[...]
            finish_front()

        @pl.when(jnp.logical_and(s > 0, s < n_tiles))
        def _():
            front_and_back()
            hbuf[...] = x_ref[...]
            finish_front()

        @pl.when(s == n_tiles)
        def _():
            mlp(hbuf[...] + output_projection())


def _layer0(x2d, seq, gm, win, cw, cb, lng, lnb, pw_bd, ps, wout, a0, p0, gf, w1, w2c, *, tile,
            l_off):
    n, d = x2d.shape
    assert seq % tile == 0 and n % seq == 0 and tile % min(ROW_BLOCK, tile) == 0
    n_tiles = n // tile
    pipelined = n_tiles > 1
    kern = functools.partial(_layer0_kernel, tile=tile, l_off=l_off, n_tiles=n_tiles,
                             tiles_per_seq=seq // tile, pipelined=pipelined)
    consts, const_specs = _split_consts((gm, win, cw, cb, lng, lnb, pw_bd, ps, wout, a0, p0, gf, w1,
                                         w2c))
    in_map, out_map = _tile_maps(n_tiles, pipelined)
    return pl.pallas_call(
        kern,
        grid=(n_tiles + pipelined,),
        in_specs=[pl.BlockSpec((tile, d), in_map)] + const_specs,
        out_specs=[pl.BlockSpec((tile, d), out_map),
                   _const_spec((A_CARRY, CONV_DIM)), _const_spec((P_CARRY, POOL_DIM))],
        out_shape=[jax.ShapeDtypeStruct(x2d.shape, F32),
                   jax.ShapeDtypeStruct((A_CARRY, CONV_DIM), F32),
                   jax.ShapeDtypeStruct((P_CARRY, POOL_DIM), F32)],
        scratch_shapes=[pltpu.VMEM((N_SLAB, A_CARRY + tile, LANES), F32),
                        pltpu.VMEM((N_SLAB, P_CARRY + tile, LANES), F32),
                        pltpu.VMEM((N_SLAB, tile, LANES), F32),
                        pltpu.VMEM((tile, CONV_DIM + POOL_DIM), BF16),
                        pltpu.VMEM((tile, d), F32)],
        compiler_params=pltpu.CompilerParams(dimension_semantics=("arbitrary",),
                                             vmem_limit_bytes=VMEM_LIMIT),
        name="layer0",
    )(x2d, *consts)


def _chunk_cumsum(tri_ref, x):
    tri = tri_ref[...]
    total = None
    for _ in range(3):
        part = x.astype(BF16)
        x = x - part.astype(F32)
        term = jnp.dot(tri, part, preferred_element_type=F32)
        total = term if total is None else total + term
    return total


def _gla_update(r0, z_scr, la_scr, state, tri_ref):
    rows = pl.ds(r0, CHUNK)
    cum = _chunk_cumsum(tri_ref, la_scr[rows, :])
    tot = cum[CHUNK - 1:CHUNK, :]
    kd = (z_scr[rows, GLA_DK:2 * GLA_DK] * jnp.exp(tot - cum)).astype(BF16)
    etot = jnp.exp(tot)
    for hh in range(GLA_HEADS):
        klo, vlo = hh * GLA_HK, 2 * GLA_DK + hh * GLA_HV
        v_h = z_scr[rows, vlo:vlo + GLA_HV].astype(BF16)
        upd = lax.dot_general(kd[:, klo:klo + GLA_HK], v_h, (((0,), (0,)), ((), ())),
                              preferred_element_type=F32)
        ecol = jnp.transpose(jnp.broadcast_to(etot[:, klo:klo + GLA_HK], (GLA_HK, GLA_HK)))
        state[hh] = state[hh] * jnp.tile(ecol, (1, GLA_HV // GLA_HK)) + upd


def _gla_output(r0, z_scr, og, state, hg_ref):
    rows = pl.ds(r0, CHUNK)
    q = (z_scr[rows, 0:GLA_DK] * (GLA_HK ** -0.5)).astype(BF16)
    glo = 2 * GLA_DK + GLA_DV
    for hh in range(GLA_HEADS):
        klo = hh * GLA_HK
        o = jnp.dot(q[:, klo:klo + GLA_HK], state[hh].astype(BF16),
                    preferred_element_type=F32)
        o = o * lax.rsqrt(jnp.mean(o * o, axis=-1, keepdims=True) + EPS) * hg_ref[...]
        gt = z_scr[rows, glo + hh * GLA_HV:glo + (hh + 1) * GLA_HV]
        og[rows, hh * GLA_HV:(hh + 1) * GLA_HV] = (o * (gt * _sigmoid(gt))).astype(BF16)


def _layer1_kernel(x_ref, gm_ref, wqkvg_ref, wr_ref, gw2_ref, gb_ref, hg_ref, wout_ref, s0_ref,
                   tri_ref, gf_ref, w1_ref, w2c_ref, fg_ref, out_ref, sfin_ref,
                   z_scr, la_scr, og, state, hbuf, *, tile, n_pad, n_tiles, tiles_per_seq,
                   final_norm, state_only):
    s = pl.program_id(0)
    t_in_seq = lax.rem(s, tiles_per_seq)
    n_chunks = tile // CHUNK

    @pl.when(t_in_seq == 0)
    def _():
        state[...] = s0_ref[...]

    def input_projection():
        u = _rms(x_ref[...], gm_ref[...]).astype(BF16)
        r = jnp.dot(u, wr_ref[...], preferred_element_type=F32)
        x = jnp.dot(r.astype(BF16), gw2_ref[...], preferred_element_type=F32) + gb_ref[...]
        la = (jnp.minimum(x, 0.0) - jnp.log1p(jnp.exp(-jnp.abs(x)))) * (1.0 / GLA_GATE_NORM)
        if n_pad:
            row = t_in_seq * tile + lax.broadcasted_iota(jnp.int32, (tile, GLA_DK), 0)
            la = jnp.where(row >= n_pad, la, 0.0)
        la_scr[...] = la
        z_scr[...] = jnp.dot(u, wqkvg_ref[...], preferred_element_type=F32)

    def output_projection():
        return hbuf[...] + jnp.dot(og[...], wout_ref[...], preferred_element_type=F32)

    def mlp_and_gla(h1, with_gla):
        uu = _rms(h1, gf_ref[...]).astype(BF16)
        acc = h1
        gla_chunks = n_chunks if with_gla else 0
        for c in range(max(N_FF, gla_chunks)):
            if c < gla_chunks:
                _gla_update(c * CHUNK, z_scr, la_scr, state, tri_ref)
            if c < N_FF:
                a = jnp.maximum(jnp.dot(uu, w1_ref[:, c * FF_CHUNK:(c + 1) * FF_CHUNK],
                                        preferred_element_type=F32), 0.0)
            if c < gla_chunks:
                _gla_output(c * CHUNK, z_scr, og, state, hg_ref)
            if c < N_FF:
                acc = acc + jnp.dot((a * a).astype(BF16), w2c_ref[c], preferred_element_type=F32)
        if final_norm:
            acc = _rms(acc, fg_ref[...])
        out_ref[...] = acc

    if state_only:
        input_projection()
        for c in range(n_chunks):
            _gla_update(c * CHUNK, z_scr, la_scr, state, tri_ref)
        out_ref[...] = x_ref[...]
    else:
        @pl.when(s == 0)
        def _():
            input_projection()

            def chunk(c, carry):
                r0 = pl.multiple_of(c * CHUNK, CHUNK)
                _gla_update(r0, z_scr, la_scr, state, tri_ref)
                _gla_output(r0, z_scr, og, state, hg_ref)
                return carry

            lax.fori_loop(0, n_chunks, chunk, 0)
            hbuf[...] = x_ref[...]

        @pl.when(s > 0)
        def _():
            h1 = output_projection()
            input_projection()
            mlp_and_gla(h1, with_gla=True)
            hbuf[...] = x_ref[...]

    @pl.when(s == n_tiles - 1)
    def _():
        sfin_ref[...] = state[...]


def _layer1(x2d, seq, gm, wqkvg, wr, gw2, gb, hg, wout, s0, tri, gf, w1, w2c, fg, *, tile, n_pad,
            final_norm, state_only=False):
    n, d = x2d.shape
    assert seq % tile == 0 and n % seq == 0 and tile % CHUNK == 0
    n_tiles = n // tile
    pipelined = not state_only
    if state_only:
        wout = w1 = w2c = jnp.zeros((8, LANES), BF16)
    kern = functools.partial(_layer1_kernel, tile=tile, n_pad=n_pad, n_tiles=n_tiles,
                             tiles_per_seq=seq // tile, final_norm=final_norm, state_only=state_only)
    consts, const_specs = _split_consts((gm, wqkvg, wr, gw2, gb, hg, wout, s0, tri, gf, w1, w2c, fg))
    state_shape = (GLA_HEADS, GLA_HK, GLA_HV)
    in_map, out_map = _tile_maps(n_tiles, pipelined)
    return pl.pallas_call(
        kern,
        grid=(n_tiles + pipelined,),
        in_specs=[pl.BlockSpec((tile, d), in_map)] + const_specs,
        out_specs=[pl.BlockSpec((tile, d), out_map), _const_spec(state_shape)],
        out_shape=[jax.ShapeDtypeStruct(x2d.shape, F32), jax.ShapeDtypeStruct(state_shape, F32)],
        scratch_shapes=[pltpu.VMEM((tile, N_QKVG), F32),
                        pltpu.VMEM((tile, GLA_DK), F32),
                        pltpu.VMEM((tile, GLA_DV), BF16),
                        pltpu.VMEM(state_shape, F32),
                        pltpu.VMEM((tile, d), F32)],
        compiler_params=pltpu.CompilerParams(dimension_semantics=("arbitrary",),
                                             vmem_limit_bytes=VMEM_LIMIT),
        name="layer1",
    )(x2d, *consts)


def kernel(x, meta_tokens, mix_norm_g, ffn_norm_g, ffn_w1, ffn_w2, cp_w_in, cp_conv_w, cp_conv_b,
           cp_ln_g, cp_ln_b, cp_pool_w, cp_pool_scale, cp_w_out, gla_w_in, gla_gate_w2, gla_gate_b,
           gla_head_g, gla_w_out, final_norm_g):
    bsz, seq, d = x.shape
    row = lambda v: v.reshape(1, -1).astype(F32)

    depth = ffn_w1.shape[0]
    w1 = _to_bf16(ffn_w1, (d, 2 * FF_CHUNK))
    w2c = _to_bf16(ffn_w2, (2 * FF_CHUNK, d)).reshape(depth, N_FF, FF_CHUNK, d)
    cp_win = _Layer(_to_bf16(cp_w_in, (d, FF_CHUNK)), 0)
    cp_wout = _Layer(_to_bf16(cp_w_out, (d, FF_CHUNK)), 0)
    pw_bd = jnp.zeros((POOL_DIM, POOL_DIM), F32)
    for gi in range(len(POOL_WINDOWS)):
        lo = gi * POOL_GROUP
        pw_bd = lax.dynamic_update_slice(pw_bd, cp_pool_w[0, gi], (lo, lo))
    pw_bd = pw_bd.astype(BF16)
    wqkvg = _Layer(_to_bf16(gla_w_in, (d, FF_CHUNK), cols=N_QKVG), 0)
    wr = jnp.pad(gla_w_in[0, :, N_QKVG:], ((0, 0), (0, LANES - GLA_GATE_RANK))).astype(BF16)
    gw2 = jnp.pad(gla_gate_w2[0], ((0, LANES - GLA_GATE_RANK), (0, 0))).astype(BF16)
    gla_wout = _Layer(_to_bf16(gla_w_out, (d, FF_CHUNK)), 0)
    tri = jnp.tril(jnp.ones((CHUNK, CHUNK), BF16))
    fg = row(final_norm_g)

    l0_args = (row(mix_norm_g[0]), cp_win, cp_conv_w[0].astype(F32), row(cp_conv_b[0]),
               row(cp_ln_g[0]), row(cp_ln_b[0]), pw_bd, row(cp_pool_scale[0]), cp_wout)
    l0_mlp = (row(ffn_norm_g[0]), _Layer(w1, 0), _Layer(w2c, 0))
    l1_args = (row(mix_norm_g[1]), wqkvg, wr, gw2, row(gla_gate_b[0]), row(gla_head_g[0]), gla_wout)
    l1_mlp = (row(ffn_norm_g[1]), _Layer(w1, 1), _Layer(w2c, 1), fg)

    hm, a_tail, p_tail = _layer0(meta_tokens.astype(F32), N_META, *l0_args,
                                 jnp.zeros((A_CARRY, CONV_DIM), F32),
                                 jnp.zeros((P_CARRY, POOL_DIM), F32), *l0_mlp, tile=N_META, l_off=0)
    hm = jnp.pad(hm, ((CHUNK - N_META, 0), (0, 0)))
    s_zero = jnp.zeros((GLA_HEADS, GLA_HK, GLA_HV), F32)
    _, s_meta = _layer1(hm, CHUNK, *l1_args, s_zero, tri, *l1_mlp, tile=CHUNK,
                        n_pad=CHUNK - N_META, final_norm=False, state_only=True)

    tile = 512
    h, _, _ = _layer0(x.astype(F32).reshape(bsz * seq, d), seq, *l0_args, a_tail, p_tail, *l0_mlp,
                      tile=tile, l_off=N_META)
    out, _ = _layer1(h, seq, *l1_args, s_meta, tri, *l1_mlp, tile=tile, n_pad=0, final_norm=True)
    return out.reshape(bsz, seq, d).astype(x.dtype)
```

```python
import functools

import jax
import jax.numpy as jnp
from jax import lax
from jax.experimental import pallas as pl
from jax.experimental.pallas import tpu as pltpu

D_MODEL = 1024
N_META = 16
CHUNK = 64
D_FF = 4 * D_MODEL
EPS = 1e-5

CONV_DIM = D_MODEL // 2
CONV_WIDTH = 31
POOL_DIM = D_MODEL // 2
POOL_WINDOWS = (2, 4, 8, 16)
POOL_GROUP = POOL_DIM // len(POOL_WINDOWS)

GLA_HEADS = 4
GLA_DK = D_MODEL // 2
GLA_DV = D_MODEL
GLA_HK = GLA_DK // GLA_HEADS
GLA_HV = GLA_DV // GLA_HEADS
GLA_GATE_RANK = 16
GLA_GATE_NORM = 16.0
N_QKVG = 2 * GLA_DK + 2 * GLA_DV

LANES = 128
N_SLAB = CONV_DIM // LANES
assert POOL_GROUP == LANES and POOL_DIM == CONV_DIM
A_CARRY = 32
P_CARRY = 16
ROW_BLOCK = 64
FF_CHUNK = 512
N_FF = D_FF // FF_CHUNK
VMEM_LIMIT = 56 * 1024 * 1024

F32 = jnp.float32
BF16 = jnp.bfloat16


def _rms(x, g):
    return x * lax.rsqrt(jnp.mean(x * x, axis=-1, keepdims=True) + EPS) * g


def _sigmoid(x):
    return 1.0 / (1.0 + jnp.exp(-x))


def _const_spec(shape):
    zeros = (0,) * len(shape)
    return pl.BlockSpec(shape, lambda *_: zeros)


class _Layer:
    def __init__(self, array, layer):
        self.array, self.layer = array, layer
        self.shape = array.shape[1:]

    def spec(self):
        index = (self.layer,) + (0,) * len(self.shape)
        return pl.BlockSpec((None,) + self.shape, lambda *_: index, pipeline_mode=pl.Buffered(1))


def _split_consts(consts):
    arrays = [c.array if isinstance(c, _Layer) else c for c in consts]
    specs = [c.spec() if isinstance(c, _Layer) else _const_spec(c.shape) for c in consts]
    return arrays, specs


def _cast_kernel(x_ref, o_ref):
    o_ref[...] = x_ref[...].astype(o_ref.dtype)


def _to_bf16(w, block, cols=None):
    layers, rows, all_cols = w.shape
    cols = all_cols if cols is None else cols
    br, bc = block
    assert rows % br == 0 and cols % bc == 0
    spec = pl.BlockSpec((1, br, bc), lambda l, i, j: (l, i, j))
    return pl.pallas_call(
        _cast_kernel,
        grid=(layers, rows // br, cols // bc),
        in_specs=[spec],
        out_specs=spec,
        out_shape=jax.ShapeDtypeStruct((layers, rows, cols), BF16),
        name="to_bf16",
    )(w)


def _tile_maps(n_tiles, pipelined):
    if not pipelined:
        return (lambda s: (s, 0)), (lambda s: (s, 0))
    return (lambda s: (jnp.minimum(s, n_tiles - 1), 0)), (lambda s: (jnp.maximum(s - 1, 0), 0))


def _dependent_zero(v):
    bits = lax.bitcast_convert_type(v, jnp.uint32)
    bits = lax.shift_right_logical(lax.shift_right_logical(bits, jnp.uint32(16)), jnp.uint32(16))
    return lax.bitcast_convert_type(bits, F32)


def _delayed(token, hops=3):
    for _ in range(hops):
        s = jnp.sum(token, axis=-1, keepdims=True)
        token = _dependent_zero(jnp.broadcast_to(s, token.shape))
    return token


def _slabs(x):
    return jnp.stack([x[:, k * LANES:(k + 1) * LANES] for k in range(N_SLAB)])


def _unslab(x):
    return jnp.concatenate([x[k] for k in range(N_SLAB)], axis=-1)


def _conv_pool_block(r0, rb, pos0, abuf, pbuf, cbuf, mix, cw_ref, cb_ref, lng_ref, lnb_ref,
                     l_off):
    for k in range(N_SLAB):
        cols = slice(k * LANES, (k + 1) * LANES)
        acc = jnp.broadcast_to(cb_ref[:, cols], (rb, LANES))
        for j in range(CONV_WIDTH):
            off = A_CARRY - (CONV_WIDTH - 1) + j
            acc = acc + cw_ref[j:j + 1, cols] * abuf[k, pl.ds(r0 + off, rb), :]
        cbuf[k, pl.ds(r0, rb), :] = acc
    conv = [cbuf[k, pl.ds(r0, rb), :] for k in range(N_SLAB)]
    mu = sum(jnp.sum(c, axis=-1, keepdims=True) for c in conv) * (1.0 / CONV_DIM)
    cen = [c - mu for c in conv]
    var = sum(jnp.sum(c * c, axis=-1, keepdims=True) for c in cen) * (1.0 / CONV_DIM)
    rstd = lax.rsqrt(var + EPS)
    for k in range(N_SLAB):
        cols = slice(k * LANES, (k + 1) * LANES)
        y = cen[k] * rstd * lng_ref[:, cols] + lnb_ref[:, cols]
        mix[pl.ds(r0, rb), cols] = (y * _sigmoid(y)).astype(BF16)
    for k, w in enumerate(POOL_WINDOWS):
        win = pbuf[k, pl.ds(r0, P_CARRY + rb), :]
        cur = win[P_CARRY:]
        ws = win
        for step in range(w.bit_length() - 1):
            ws = ws + pltpu.roll(ws, 1 << step, axis=0)
        ws = ws[P_CARRY:]
        if l_off + 1 >= w:
            mean = ws * (1.0 / w)
        else:
            pos = pos0 + r0 + lax.broadcasted_iota(jnp.int32, (rb, LANES), 0)
            mean = ws / jnp.minimum(pos + 1, w).astype(F32)
        mix[pl.ds(r0, rb), CONV_DIM + k * LANES:CONV_DIM + (k + 1) * LANES] = (mean - cur).astype(BF16)


def _layer0_kernel(x_ref, gm_ref, win_ref, cw_ref, cb_ref, lng_ref, lnb_ref, pw_ref, ps_ref, wout_ref,
                   a0_ref, p0_ref, gf_ref, w1_ref, w2c_ref, out_ref, atail_ref, ptail_ref,
                   abuf, pbuf, cbuf, mix, hbuf, *, tile, l_off, n_tiles, tiles_per_seq, pipelined):
    s = pl.program_id(0)
    t_in_seq = lax.rem(s, tiles_per_seq)
    rb = min(ROW_BLOCK, tile)
    pos0 = l_off + t_in_seq * tile

    def conv_blocks():
        for i in range(tile // rb):
            _conv_pool_block(i * rb, rb, pos0, abuf, pbuf, cbuf, mix, cw_ref, cb_ref, lng_ref,
                             lnb_ref, l_off)

    @pl.when(t_in_seq == 0)
    def _():
        abuf[:, 0:A_CARRY, :] = _slabs(a0_ref[...])
        pbuf[:, 0:P_CARRY, :] = _slabs(p0_ref[...])

    def input_projection():
        u = _rms(x_ref[...], gm_ref[...]).astype(BF16)
        z = jnp.dot(u, win_ref[...], preferred_element_type=F32)
        return z[:, :CONV_DIM] * _sigmoid(z[:, CONV_DIM:2 * CONV_DIM]), z[:, 2 * CONV_DIM:]

    def output_projection():
        pooled = jnp.dot(mix[:, CONV_DIM:], pw_ref[...], preferred_element_type=F32) * ps_ref[...]
        y = jnp.dot(mix[:, 0:CONV_DIM], wout_ref[0:CONV_DIM, :], preferred_element_type=F32)
        return y + jnp.dot(pooled.astype(BF16), wout_ref[CONV_DIM:, :], preferred_element_type=F32)

    def mlp(h1):
        uu = _rms(h1, gf_ref[...]).astype(BF16)
        acc = h1
        for c in range(N_FF):
            a = jnp.maximum(jnp.dot(uu, w1_ref[:, c * FF_CHUNK:(c + 1) * FF_CHUNK],
                                    preferred_element_type=F32), 0.0)
            acc = acc + jnp.dot((a * a).astype(BF16), w2c_ref[c], preferred_element_type=F32)
        out_ref[...] = acc

    def finish_front():
        a_tail = abuf[:, tile:tile + A_CARRY, :]
        p_tail = pbuf[:, tile:tile + P_CARRY, :]
        abuf[:, 0:A_CARRY, :] = a_tail
        pbuf[:, 0:P_CARRY, :] = p_tail

        @pl.when(s == n_tiles - 1)
        def _():
            atail_ref[...] = _unslab(a_tail)
            ptail_ref[...] = _unslab(p_tail)

    def front_only():
        glu, pin = input_projection()
        abuf[:, A_CARRY:A_CARRY + tile, :] = _slabs(glu)
        pbuf[:, P_CARRY:P_CARRY + tile, :] = _slabs(pin)
        conv_blocks()

    def front_and_back():
        y = output_projection()
        glu, pin = input_projection()
        h1 = hbuf[...] + y
        token = _dependent_zero(glu[0:8, 0:LANES])
        for i in range(tile // rb):
            rows = slice(i * rb, (i + 1) * rb)
            for k in range(N_SLAB):
                zero = jnp.tile(token, (rb // 8, 1))
                cols = slice(k * LANES, (k + 1) * LANES)
                abuf[k, A_CARRY + i * rb:A_CARRY + (i + 1) * rb, :] = glu[rows, cols] + zero
                pbuf[k, P_CARRY + i * rb:P_CARRY + (i + 1) * rb, :] = pin[rows, cols] + zero
                token = _delayed(token)
        mlp(h1)
        conv_blocks()

    if not pipelined:
        front_only()
        mlp(x_ref[...] + output_projection())
        finish_front()
    else:
        @pl.when(s == 0)
        def _():
            front_only()
            hbuf[...] = x_ref[...]
            finish_front()

        @pl.when(jnp.logical_and(s > 0, s < n_tiles))
        def _():
            front_and_back()
            hbuf[...] = x_ref[...]
            finish_front()

        @pl.when(s == n_tiles)
        def _():
            mlp(hbuf[...] + output_projection())


def _layer0(x2d, seq, gm, win, cw, cb, lng, lnb, pw_bd, ps, wout, a0, p0, gf, w1, w2c, *, tile,
            l_off):
    n, d = x2d.shape
    assert seq % tile == 0 and n % seq == 0 and tile % min(ROW_BLOCK, tile) == 0
    n_tiles = n // tile
    pipelined = n_tiles > 1
    kern = functools.partial(_layer0_kernel, tile=tile, l_off=l_off, n_tiles=n_tiles,
                             tiles_per_seq=seq // tile, pipelined=pipelined)
    consts, const_specs = _split_consts((gm, win, cw, cb, lng, lnb, pw_bd, ps, wout, a0, p0, gf, w1,
                                         w2c))
    in_map, out_map = _tile_maps(n_tiles, pipelined)
    return pl.pallas_call(
        kern,
        grid=(n_tiles + pipelined,),
        in_specs=[pl.BlockSpec((tile, d), in_map)] + const_specs,
        out_specs=[pl.BlockSpec((tile, d), out_map),
                   _const_spec((A_CARRY, CONV_DIM)), _const_spec((P_CARRY, POOL_DIM))],
        out_shape=[jax.ShapeDtypeStruct(x2d.shape, F32),
                   jax.ShapeDtypeStruct((A_CARRY, CONV_DIM), F32),
                   jax.ShapeDtypeStruct((P_CARRY, POOL_DIM), F32)],
        scratch_shapes=[pltpu.VMEM((N_SLAB, A_CARRY + tile, LANES), F32),
                        pltpu.VMEM((N_SLAB, P_CARRY + tile, LANES), F32),
                        pltpu.VMEM((N_SLAB, tile, LANES), F32),
                        pltpu.VMEM((tile, CONV_DIM + POOL_DIM), BF16),
                        pltpu.VMEM((tile, d), F32)],
        compiler_params=pltpu.CompilerParams(dimension_semantics=("arbitrary",),
                                             vmem_limit_bytes=VMEM_LIMIT),
        name="layer0",
    )(x2d, *consts)


def _chunk_cumsum(tri_ref, x):
    tri = tri_ref[...]
    total = None
    for _ in range(3):
        part = x.astype(BF16)
        x = x - part.astype(F32)
        term = jnp.dot(tri, part, preferred_element_type=F32)
        total = term if total is None else total + term
    return total


def _gla_decay(la_scr, dec_scr, etot_scr, tri_ref, n_chunks):
    for c in range(n_chunks):
        rows = slice(c * CHUNK, (c + 1) * CHUNK)
        cum = _chunk_cumsum(tri_ref, la_scr[rows, :])
        tot = cum[CHUNK - 1:CHUNK, :]
        dec_scr[rows, :] = jnp.exp(tot - cum)
        etot_scr[c] = jnp.broadcast_to(jnp.exp(tot), (8, GLA_DK))


def _gla_update(c, r0, z_scr, dec_scr, etot_scr, state):
    rows = pl.ds(r0, CHUNK)
    kd = (z_scr[rows, GLA_DK:2 * GLA_DK] * dec_scr[rows, :]).astype(BF16)
    etot = etot_scr[c][0:1, :]
    for hh in range(GLA_HEADS):
        klo, vlo = hh * GLA_HK, 2 * GLA_DK + hh * GLA_HV
        v_h = z_scr[rows, vlo:vlo + GLA_HV].astype(BF16)
        upd = lax.dot_general(kd[:, klo:klo + GLA_HK], v_h, (((0,), (0,)), ((), ())),
                              preferred_element_type=F32)
        ecol = jnp.transpose(jnp.broadcast_to(etot[:, klo:klo + GLA_HK], (GLA_HK, GLA_HK)))
        state[hh] = state[hh] * jnp.tile(ecol, (1, GLA_HV // GLA_HK)) + upd


def _gla_output(r0, z_scr, og, state, hg_ref):
    rows = pl.ds(r0, CHUNK)
    q = (z_scr[rows, 0:GLA_DK] * (GLA_HK ** -0.5)).astype(BF16)
    glo = 2 * GLA_DK + GLA_DV
    for hh in range(GLA_HEADS):
        klo = hh * GLA_HK
        o = jnp.dot(q[:, klo:klo + GLA_HK], state[hh].astype(BF16),
                    preferred_element_type=F32)
        o = o * lax.rsqrt(jnp.mean(o * o, axis=-1, keepdims=True) + EPS) * hg_ref[...]
        gt = z_scr[rows, glo + hh * GLA_HV:glo + (hh + 1) * GLA_HV]
        og[rows, hh * GLA_HV:(hh + 1) * GLA_HV] = (o * (gt * _sigmoid(gt))).astype(BF16)


def _layer1_kernel(x_ref, gm_ref, wqkvg_ref, wr_ref, gw2_ref, gb_ref, hg_ref, wout_ref, s0_ref,
                   tri_ref, gf_ref, w1_ref, w2c_ref, fg_ref, out_ref, sfin_ref,
                   z_scr, la_scr, dec_scr, etot_scr, og, state, hbuf, *, tile, n_pad, n_tiles,
                   tiles_per_seq, final_norm, state_only):
    s = pl.program_id(0)
    t_in_seq = lax.rem(s, tiles_per_seq)
    n_chunks = tile // CHUNK

    @pl.when(t_in_seq == 0)
    def _():
        state[...] = s0_ref[...]

    def input_projection():
        u = _rms(x_ref[...], gm_ref[...]).astype(BF16)
        r = jnp.dot(u, wr_ref[...], preferred_element_type=F32)
        x = jnp.dot(r.astype(BF16), gw2_ref[...], preferred_element_type=F32) + gb_ref[...]
        la = (jnp.minimum(x, 0.0) - jnp.log1p(jnp.exp(-jnp.abs(x)))) * (1.0 / GLA_GATE_NORM)
        if n_pad:
            row = t_in_seq * tile + lax.broadcasted_iota(jnp.int32, (tile, GLA_DK), 0)
            la = jnp.where(row >= n_pad, la, 0.0)
        la_scr[...] = la
        _gla_decay(la_scr, dec_scr, etot_scr, tri_ref, n_chunks)
        z_scr[...] = jnp.dot(u, wqkvg_ref[...], preferred_element_type=F32)

    def output_projection():
        return hbuf[...] + jnp.dot(og[...], wout_ref[...], preferred_element_type=F32)

    def mlp_and_gla(h1, with_gla):
        uu = _rms(h1, gf_ref[...]).astype(BF16)
        acc = h1
        gla_chunks = n_chunks if with_gla else 0
        for c in range(max(N_FF, gla_chunks)):
            if c < N_FF:
                a = jnp.maximum(jnp.dot(uu, w1_ref[:, c * FF_CHUNK:(c + 1) * FF_CHUNK],
                                        preferred_element_type=F32), 0.0)
            if c < gla_chunks:
                _gla_update(c, c * CHUNK, z_scr, dec_scr, etot_scr, state)
            if c < N_FF:
                acc = acc + jnp.dot((a * a).astype(BF16), w2c_ref[c], preferred_element_type=F32)
            if c < gla_chunks:
                _gla_output(c * CHUNK, z_scr, og, state, hg_ref)
        if final_norm:
            acc = _rms(acc, fg_ref[...])
        out_ref[...] = acc

    if state_only:
        input_projection()
        for c in range(n_chunks):
            _gla_update(c, c * CHUNK, z_scr, dec_scr, etot_scr, state)
        out_ref[...] = x_ref[...]
    else:
        @pl.when(s == 0)
        def _():
            input_projection()
            for c in range(n_chunks):
                _gla_update(c, c * CHUNK, z_scr, dec_scr, etot_scr, state)
                _gla_output(c * CHUNK, z_scr, og, state, hg_ref)
            hbuf[...] = x_ref[...]

        @pl.when(jnp.logical_and(s > 0, s < n_tiles))
        def _():
            h1 = output_projection()
            input_projection()
            mlp_and_gla(h1, with_gla=True)
            hbuf[...] = x_ref[...]

        @pl.when(s == n_tiles)
        def _():
            mlp_and_gla(output_projection(), with_gla=False)

    @pl.when(s == n_tiles - 1)
    def _():
        sfin_ref[...] = state[...]


def _layer1(x2d, seq, gm, wqkvg, wr, gw2, gb, hg, wout, s0, tri, gf, w1, w2c, fg, *, tile, n_pad,
            final_norm, state_only=False):
    n, d = x2d.shape
    assert seq % tile == 0 and n % seq == 0 and tile % CHUNK == 0
    n_tiles = n // tile
    pipelined = not state_only
    if state_only:
        wout = w1 = w2c = jnp.zeros((8, LANES), BF16)
    kern = functools.partial(_layer1_kernel, tile=tile, n_pad=n_pad, n_tiles=n_tiles,
                             tiles_per_seq=seq // tile, final_norm=final_norm, state_only=state_only)
    consts, const_specs = _split_consts((gm, wqkvg, wr, gw2, gb, hg, wout, s0, tri, gf, w1, w2c, fg))
    state_shape = (GLA_HEADS, GLA_HK, GLA_HV)
    in_map, out_map = _tile_maps(n_tiles, pipelined)
    return pl.pallas_call(
        kern,
        grid=(n_tiles + pipelined,),
        in_specs=[pl.BlockSpec((tile, d), in_map)] + const_specs,
        out_specs=[pl.BlockSpec((tile, d), out_map), _const_spec(state_shape)],
        out_shape=[jax.ShapeDtypeStruct(x2d.shape, F32), jax.ShapeDtypeStruct(state_shape, F32)],
        scratch_shapes=[pltpu.VMEM((tile, N_QKVG), F32),
                        pltpu.VMEM((tile, GLA_DK), F32),
                        pltpu.VMEM((tile, GLA_DK), F32),
                        pltpu.VMEM((tile // CHUNK, 8, GLA_DK), F32),
                        pltpu.VMEM((tile, GLA_DV), BF16),
                        pltpu.VMEM(state_shape, F32),
                        pltpu.VMEM((tile, d), F32)],
        compiler_params=pltpu.CompilerParams(dimension_semantics=("arbitrary",),
                                             vmem_limit_bytes=VMEM_LIMIT),
        name="layer1",
    )(x2d, *consts)


def kernel(x, meta_tokens, mix_norm_g, ffn_norm_g, ffn_w1, ffn_w2, cp_w_in, cp_conv_w, cp_conv_b,
           cp_ln_g, cp_ln_b, cp_pool_w, cp_pool_scale, cp_w_out, gla_w_in, gla_gate_w2, gla_gate_b,
           gla_head_g, gla_w_out, final_norm_g):
    bsz, seq, d = x.shape
    row = lambda v: v.reshape(1, -1).astype(F32)

    depth = ffn_w1.shape[0]
    w1 = _to_bf16(ffn_w1, (d, 2 * FF_CHUNK))
    w2c = _to_bf16(ffn_w2, (2 * FF_CHUNK, d)).reshape(depth, N_FF, FF_CHUNK, d)
    cp_win = _Layer(_to_bf16(cp_w_in, (d, FF_CHUNK)), 0)
    cp_wout = _Layer(_to_bf16(cp_w_out, (d, FF_CHUNK)), 0)
    pw_bd = jnp.zeros((POOL_DIM, POOL_DIM), F32)
    for gi in range(len(POOL_WINDOWS)):
        lo = gi * POOL_GROUP
        pw_bd = lax.dynamic_update_slice(pw_bd, cp_pool_w[0, gi], (lo, lo))
    pw_bd = pw_bd.astype(BF16)
    wqkvg = _Layer(_to_bf16(gla_w_in, (d, FF_CHUNK), cols=N_QKVG), 0)
    wr = jnp.pad(gla_w_in[0, :, N_QKVG:], ((0, 0), (0, LANES - GLA_GATE_RANK))).astype(BF16)
    gw2 = jnp.pad(gla_gate_w2[0], ((0, LANES - GLA_GATE_RANK), (0, 0))).astype(BF16)
    gla_wout = _Layer(_to_bf16(gla_w_out, (d, FF_CHUNK)), 0)
    tri = jnp.tril(jnp.ones((CHUNK, CHUNK), BF16))
    fg = row(final_norm_g)

    l0_args = (row(mix_norm_g[0]), cp_win, cp_conv_w[0].astype(F32), row(cp_conv_b[0]),
               row(cp_ln_g[0]), row(cp_ln_b[0]), pw_bd, row(cp_pool_scale[0]), cp_wout)
    l0_mlp = (row(ffn_norm_g[0]), _Layer(w1, 0), _Layer(w2c, 0))
    l1_args = (row(mix_norm_g[1]), wqkvg, wr, gw2, row(gla_gate_b[0]), row(gla_head_g[0]), gla_wout)
    l1_mlp = (row(ffn_norm_g[1]), _Layer(w1, 1), _Layer(w2c, 1), fg)

    hm, a_tail, p_tail = _layer0(meta_tokens.astype(F32), N_META, *l0_args,
                                 jnp.zeros((A_CARRY, CONV_DIM), F32),
                                 jnp.zeros((P_CARRY, POOL_DIM), F32), *l0_mlp, tile=N_META, l_off=0)
    hm = jnp.pad(hm, ((CHUNK - N_META, 0), (0, 0)))
    s_zero = jnp.zeros((GLA_HEADS, GLA_HK, GLA_HV), F32)
    _, s_meta = _layer1(hm, CHUNK, *l1_args, s_zero, tri, *l1_mlp, tile=CHUNK,
                        n_pad=CHUNK - N_META, final_norm=False, state_only=True)

    tile = 512
    h, _, _ = _layer0(x.astype(F32).reshape(bsz * seq, d), seq, *l0_args, a_tail, p_tail, *l0_mlp,
                      tile=tile, l_off=N_META)
    out, _ = _layer1(h, seq, *l1_args, s_meta, tri, *l1_mlp, tile=tile, n_pad=0, final_norm=True)
    return out.reshape(bsz, seq, d).astype(x.dtype)
```

```python
import functools

import jax
import jax.numpy as jnp
from jax import lax
from jax.experimental import pallas as pl
from jax.experimental.pallas import tpu as pltpu

D_MODEL = 1024
N_META = 16
CHUNK = 64
D_FF = 4 * D_MODEL
EPS = 1e-5

CONV_DIM = D_MODEL // 2
CONV_WIDTH = 31
POOL_DIM = D_MODEL // 2
POOL_WINDOWS = (2, 4, 8, 16)
POOL_GROUP = POOL_DIM // len(POOL_WINDOWS)

GLA_HEADS = 4
GLA_DK = D_MODEL // 2
GLA_DV = D_MODEL
GLA_HK = GLA_DK // GLA_HEADS
GLA_HV = GLA_DV // GLA_HEADS
GLA_GATE_RANK = 16
GLA_GATE_NORM = 16.0
N_QKVG = 2 * GLA_DK + 2 * GLA_DV

LANES = 128
N_SLAB = CONV_DIM // LANES
assert POOL_GROUP == LANES and POOL_DIM == CONV_DIM
A_CARRY = 32
P_CARRY = 16
ROW_BLOCK = 64
FF_CHUNK = 512
N_FF = D_FF // FF_CHUNK
VMEM_LIMIT = 56 * 1024 * 1024

F32 = jnp.float32
BF16 = jnp.bfloat16


def _rms(x, g):
    return x * lax.rsqrt(jnp.mean(x * x, axis=-1, keepdims=True) + EPS) * g


def _sigmoid(x):
    return 1.0 / (1.0 + jnp.exp(-x))


def _const_spec(shape):
    zeros = (0,) * len(shape)
    return pl.BlockSpec(shape, lambda *_: zeros)


class _Layer:
    def __init__(self, array, layer):
        self.array, self.layer = array, layer
        self.shape = array.shape[1:]

    def spec(self):
        index = (self.layer,) + (0,) * len(self.shape)
        return pl.BlockSpec((None,) + self.shape, lambda *_: index, pipeline_mode=pl.Buffered(1))


def _split_consts(consts):
    arrays = [c.array if isinstance(c, _Layer) else c for c in consts]
    specs = [c.spec() if isinstance(c, _Layer) else _const_spec(c.shape) for c in consts]
    return arrays, specs


def _cast_kernel(x_ref, o_ref):
    o_ref[...] = x_ref[...].astype(o_ref.dtype)


def _to_bf16(w, block):
    layers, rows, cols = w.shape
    br, bc = block
    assert rows % br == 0 and cols % bc == 0
    spec = pl.BlockSpec((1, br, bc), lambda l, i, j: (l, i, j))
    return pl.pallas_call(
        _cast_kernel,
        grid=(layers, rows // br, cols // bc),
        in_specs=[spec],
        out_specs=spec,
        out_shape=jax.ShapeDtypeStruct((layers, rows, cols), BF16),
        name="to_bf16",
    )(w)


def _tile_maps(n_tiles, pipelined):
    if not pipelined:
        return (lambda s: (s, 0)), (lambda s: (s, 0))
    return (lambda s: (jnp.minimum(s, n_tiles - 1), 0)), (lambda s: (jnp.maximum(s - 1, 0), 0))


def _dependent_zero(v):
    bits = lax.bitcast_convert_type(v, jnp.uint32)
    bits = lax.shift_right_logical(lax.shift_right_logical(bits, jnp.uint32(16)), jnp.uint32(16))
    return lax.bitcast_convert_type(bits, F32)


def _delayed(token, hops=3):
    for _ in range(hops):
        s = jnp.sum(token, axis=-1, keepdims=True)
        token = _dependent_zero(jnp.broadcast_to(s, token.shape))
    return token


def _slabs(x):
    return jnp.stack([x[:, k * LANES:(k + 1) * LANES] for k in range(N_SLAB)])


def _unslab(x):
    return jnp.concatenate([x[k] for k in range(N_SLAB)], axis=-1)


def _conv_pool_block(r0, rb, pos0, abuf, pbuf, cbuf, mix, cw_ref, cb_ref, lng_ref, lnb_ref,
                     l_off):
    for k in range(N_SLAB):
        cols = slice(k * LANES, (k + 1) * LANES)
        acc = jnp.broadcast_to(cb_ref[:, cols], (rb, LANES))
        for j in range(CONV_WIDTH):
            off = A_CARRY - (CONV_WIDTH - 1) + j
            acc = acc + cw_ref[j:j + 1, cols] * abuf[k, pl.ds(r0 + off, rb), :]
        cbuf[k, pl.ds(r0, rb), :] = acc
    conv = [cbuf[k, pl.ds(r0, rb), :] for k in range(N_SLAB)]
    mu = sum(jnp.sum(c, axis=-1, keepdims=True) for c in conv) * (1.0 / CONV_DIM)
    cen = [c - mu for c in conv]
    var = sum(jnp.sum(c * c, axis=-1, keepdims=True) for c in cen) * (1.0 / CONV_DIM)
    rstd = lax.rsqrt(var + EPS)
    for k in range(N_SLAB):
        cols = slice(k * LANES, (k + 1) * LANES)
        y = cen[k] * rstd * lng_ref[:, cols] + lnb_ref[:, cols]
        mix[pl.ds(r0, rb), cols] = (y * _sigmoid(y)).astype(BF16)
    for k, w in enumerate(POOL_WINDOWS):
        win = pbuf[k, pl.ds(r0, P_CARRY + rb), :]
        cur = win[P_CARRY:]
        ws = win
        for step in range(w.bit_length() - 1):
            ws = ws + pltpu.roll(ws, 1 << step, axis=0)
        ws = ws[P_CARRY:]
        if l_off + 1 >= w:
            mean = ws * (1.0 / w)
        else:
            pos = pos0 + r0 + lax.broadcasted_iota(jnp.int32, (rb, LANES), 0)
            mean = ws / jnp.minimum(pos + 1, w).astype(F32)
        mix[pl.ds(r0, rb), CONV_DIM + k * LANES:CONV_DIM + (k + 1) * LANES] = (mean - cur).astype(BF16)


def _layer0_kernel(x_ref, gm_ref, win_ref, cw_ref, cb_ref, lng_ref, lnb_ref, pw_ref, ps_ref, wout_ref,
                   a0_ref, p0_ref, gf_ref, w1_ref, w2c_ref, out_ref, atail_ref, ptail_ref,
                   abuf, pbuf, cbuf, mix, hbuf, *, tile, l_off, n_tiles, tiles_per_seq, pipelined):
    s = pl.program_id(0)
    t_in_seq = lax.rem(s, tiles_per_seq)
    rb = min(ROW_BLOCK, tile)
    pos0 = l_off + t_in_seq * tile

    def conv_blocks():
        for i in range(tile // rb):
            _conv_pool_block(i * rb, rb, pos0, abuf, pbuf, cbuf, mix, cw_ref, cb_ref, lng_ref,
                             lnb_ref, l_off)

    @pl.when(t_in_seq == 0)
    def _():
        abuf[:, 0:A_CARRY, :] = _slabs(a0_ref[...])
        pbuf[:, 0:P_CARRY, :] = _slabs(p0_ref[...])

    def input_projection():
        u = _rms(x_ref[...], gm_ref[...]).astype(BF16)
        z = jnp.dot(u, win_ref[...], preferred_element_type=F32)
        return z[:, :CONV_DIM] * _sigmoid(z[:, CONV_DIM:2 * CONV_DIM]), z[:, 2 * CONV_DIM:]

    def output_projection():
        pooled = jnp.dot(mix[:, CONV_DIM:], pw_ref[...], preferred_element_type=F32) * ps_ref[...]
        y = jnp.dot(mix[:, 0:CONV_DIM], wout_ref[0:CONV_DIM, :], preferred_element_type=F32)
        return y + jnp.dot(pooled.astype(BF16), wout_ref[CONV_DIM:, :], preferred_element_type=F32)

    def mlp(h1):
        uu = _rms(h1, gf_ref[...]).astype(BF16)
        acc = h1
        for c in range(N_FF):
            a = jnp.maximum(jnp.dot(uu, w1_ref[:, c * FF_CHUNK:(c + 1) * FF_CHUNK],
                                    preferred_element_type=F32), 0.0)
            acc = acc + jnp.dot((a * a).astype(BF16), w2c_ref[c], preferred_element_type=F32)
        out_ref[...] = acc

    def finish_front():
        a_tail = abuf[:, tile:tile + A_CARRY, :]
        p_tail = pbuf[:, tile:tile + P_CARRY, :]
        abuf[:, 0:A_CARRY, :] = a_tail
        pbuf[:, 0:P_CARRY, :] = p_tail

        @pl.when(s == n_tiles - 1)
        def _():
            atail_ref[...] = _unslab(a_tail)
            ptail_ref[...] = _unslab(p_tail)

    def front_only():
        glu, pin = input_projection()
        abuf[:, A_CARRY:A_CARRY + tile, :] = _slabs(glu)
        pbuf[:, P_CARRY:P_CARRY + tile, :] = _slabs(pin)
        conv_blocks()

    def front_and_back():
        y = output_projection()
        glu, pin = input_projection()
        h1 = hbuf[...] + y
        token = _dependent_zero(glu[0:8, 0:LANES])
        for i in range(tile // rb):
            rows = slice(i * rb, (i + 1) * rb)
            for k in range(N_SLAB):
                zero = jnp.tile(token, (rb // 8, 1))
                cols = slice(k * LANES, (k + 1) * LANES)
                abuf[k, A_CARRY + i * rb:A_CARRY + (i + 1) * rb, :] = glu[rows, cols] + zero
                pbuf[k, P_CARRY + i * rb:P_CARRY + (i + 1) * rb, :] = pin[rows, cols] + zero
                token = _delayed(token)
        mlp(h1)
        conv_blocks()

    if not pipelined:
        front_only()
        mlp(x_ref[...] + output_projection())
        finish_front()
    else:
        @pl.when(s == 0)
        def _():
            front_only()
            hbuf[...] = x_ref[...]
            finish_front()

        @pl.when(jnp.logical_and(s > 0, s < n_tiles))
        def _():
            front_and_back()
            hbuf[...] = x_ref[...]
            finish_front()

        @pl.when(s == n_tiles)
        def _():
            mlp(hbuf[...] + output_projection())


def _layer0(x2d, seq, gm, win, cw, cb, lng, lnb, pw_bd, ps, wout, a0, p0, gf, w1, w2c, *, tile,
            l_off):
    n, d = x2d.shape
    assert seq % tile == 0 and n % seq == 0 and tile % min(ROW_BLOCK, tile) == 0
    n_tiles = n // tile
    pipelined = n_tiles > 1
    kern = functools.partial(_layer0_kernel, tile=tile, l_off=l_off, n_tiles=n_tiles,
                             tiles_per_seq=seq // tile, pipelined=pipelined)
    consts, const_specs = _split_consts((gm, win, cw, cb, lng, lnb, pw_bd, ps, wout, a0, p0, gf, w1,
                                         w2c))
    in_map, out_map = _tile_maps(n_tiles, pipelined)
    return pl.pallas_call(
        kern,
        grid=(n_tiles + pipelined,),
        in_specs=[pl.BlockSpec((tile, d), in_map)] + const_specs,
        out_specs=[pl.BlockSpec((tile, d), out_map),
                   _const_spec((A_CARRY, CONV_DIM)), _const_spec((P_CARRY, POOL_DIM))],
        out_shape=[jax.ShapeDtypeStruct(x2d.shape, F32),
                   jax.ShapeDtypeStruct((A_CARRY, CONV_DIM), F32),
                   jax.ShapeDtypeStruct((P_CARRY, POOL_DIM), F32)],
        scratch_shapes=[pltpu.VMEM((N_SLAB, A_CARRY + tile, LANES), F32),
                        pltpu.VMEM((N_SLAB, P_CARRY + tile, LANES), F32),
                        pltpu.VMEM((N_SLAB, tile, LANES), F32),
                        pltpu.VMEM((tile, CONV_DIM + POOL_DIM), BF16),
                        pltpu.VMEM((tile, d), F32)],
        compiler_params=pltpu.CompilerParams(dimension_semantics=("arbitrary",),
                                             vmem_limit_bytes=VMEM_LIMIT),
        name="layer0",
    )(x2d, *consts)


def _chunk_cumsum(tri_ref, x):
    tri = tri_ref[...]
    total = None
    for _ in range(3):
        part = x.astype(BF16)
        x = x - part.astype(F32)
        term = jnp.dot(tri, part, preferred_element_type=F32)
        total = term if total is None else total + term
    return total


def _gla_decay(la_scr, dec_scr, etot_scr, tri_ref, n_chunks):
    for c in range(n_chunks):
        rows = slice(c * CHUNK, (c + 1) * CHUNK)
        cum = _chunk_cumsum(tri_ref, la_scr[rows, :])
        tot = cum[CHUNK - 1:CHUNK, :]
        dec_scr[rows, :] = jnp.exp(tot - cum)
        etot_scr[c] = jnp.broadcast_to(jnp.exp(tot), (8, GLA_DK))


def _gla_update(c, r0, z_scr, dec_scr, etot_scr, state):
    rows = pl.ds(r0, CHUNK)
    kd = (z_scr[rows, GLA_DK:2 * GLA_DK] * dec_scr[rows, :]).astype(BF16)
    etot = etot_scr[c][0:1, :]
    for hh in range(GLA_HEADS):
        klo, vlo = hh * GLA_HK, 2 * GLA_DK + hh * GLA_HV
        v_h = z_scr[rows, vlo:vlo + GLA_HV].astype(BF16)
        upd = lax.dot_general(kd[:, klo:klo + GLA_HK], v_h, (((0,), (0,)), ((), ())),
                              preferred_element_type=F32)
        ecol = jnp.transpose(jnp.broadcast_to(etot[:, klo:klo + GLA_HK], (GLA_HK, GLA_HK)))
        state[hh] = state[hh] * jnp.tile(ecol, (1, GLA_HV // GLA_HK)) + upd


def _gla_output(r0, z_scr, og, state, hg_ref):
    rows = pl.ds(r0, CHUNK)
    q = (z_scr[rows, 0:GLA_DK] * (GLA_HK ** -0.5)).astype(BF16)
    glo = 2 * GLA_DK + GLA_DV
    for hh in range(GLA_HEADS):
        klo = hh * GLA_HK
        o = jnp.dot(q[:, klo:klo + GLA_HK], state[hh].astype(BF16),
                    preferred_element_type=F32)
        o = o * lax.rsqrt(jnp.mean(o * o, axis=-1, keepdims=True) + EPS) * hg_ref[...]
        gt = z_scr[rows, glo + hh * GLA_HV:glo + (hh + 1) * GLA_HV]
        og[rows, hh * GLA_HV:(hh + 1) * GLA_HV] = (o * (gt * _sigmoid(gt))).astype(BF16)


def _layer1_kernel(x_ref, gm_ref, wqkvg_ref, wr_ref, gw2_ref, gb_ref, hg_ref, wout_ref, s0_ref,
                   tri_ref, gf_ref, w1_ref, w2c_ref, fg_ref, out_ref, sfin_ref,
                   z_scr, la_scr, dec_scr, etot_scr, og, state, hbuf, *, tile, n_pad, n_tiles,
                   tiles_per_seq, final_norm, state_only):
    s = pl.program_id(0)
    t_in_seq = lax.rem(s, tiles_per_seq)
    n_chunks = tile // CHUNK

    @pl.when(t_in_seq == 0)
    def _():
        state[...] = s0_ref[...]

    def input_projection():
        u = _rms(x_ref[...], gm_ref[...]).astype(BF16)
        r = jnp.dot(u, wr_ref[...], preferred_element_type=F32)
        x = jnp.dot(r.astype(BF16), gw2_ref[...], preferred_element_type=F32) + gb_ref[...]
        la = (jnp.minimum(x, 0.0) - jnp.log1p(jnp.exp(-jnp.abs(x)))) * (1.0 / GLA_GATE_NORM)
        if n_pad:
            row = t_in_seq * tile + lax.broadcasted_iota(jnp.int32, (tile, GLA_DK), 0)
            la = jnp.where(row >= n_pad, la, 0.0)
        la_scr[...] = la
        z_scr[...] = jnp.dot(u, wqkvg_ref[...], preferred_element_type=F32)
        _gla_decay(la_scr, dec_scr, etot_scr, tri_ref, n_chunks)

    def output_projection():
        return hbuf[...] + jnp.dot(og[...], wout_ref[...], preferred_element_type=F32)

    def mlp_and_gla(h1, with_gla):
        uu = _rms(h1, gf_ref[...]).astype(BF16)
        acc = h1
        gla_chunks = n_chunks if with_gla else 0
        for c in range(max(N_FF, gla_chunks)):
            if c < N_FF:
                a = jnp.maximum(jnp.dot(uu, w1_ref[:, c * FF_CHUNK:(c + 1) * FF_CHUNK],
                                        preferred_element_type=F32), 0.0)
            if c < gla_chunks:
                _gla_update(c, c * CHUNK, z_scr, dec_scr, etot_scr, state)
            if c < N_FF:
                acc = acc + jnp.dot((a * a).astype(BF16), w2c_ref[c], preferred_element_type=F32)
            if c < gla_chunks:
                _gla_output(c * CHUNK, z_scr, og, state, hg_ref)
        if final_norm:
            acc = _rms(acc, fg_ref[...])
        out_ref[...] = acc

    if state_only:
        input_projection()
        for c in range(n_chunks):
            _gla_update(c, c * CHUNK, z_scr, dec_scr, etot_scr, state)
        out_ref[...] = x_ref[...]
    else:
        @pl.when(s == 0)
        def _():
            input_projection()
            for c in range(n_chunks):
                _gla_update(c, c * CHUNK, z_scr, dec_scr, etot_scr, state)
                _gla_output(c * CHUNK, z_scr, og, state, hg_ref)
            hbuf[...] = x_ref[...]

        @pl.when(jnp.logical_and(s > 0, s < n_tiles))
        def _():
            h1 = output_projection()
            input_projection()
            mlp_and_gla(h1, with_gla=True)
            hbuf[...] = x_ref[...]

        @pl.when(s == n_tiles)
        def _():
            mlp_and_gla(output_projection(), with_gla=False)

    @pl.when(s == n_tiles - 1)
    def _():
        sfin_ref[...] = state[...]


def _layer1(x2d, seq, gm, wqkvg, wr, gw2, gb, hg, wout, s0, tri, gf, w1, w2c, fg, *, tile, n_pad,
            final_norm, state_only=False):
    n, d = x2d.shape
    assert seq % tile == 0 and n % seq == 0 and tile % CHUNK == 0
    n_tiles = n // tile
    pipelined = not state_only
    if state_only:
        wout = w1 = w2c = jnp.zeros((8, LANES), BF16)
    kern = functools.partial(_layer1_kernel, tile=tile, n_pad=n_pad, n_tiles=n_tiles,
                             tiles_per_seq=seq // tile, final_norm=final_norm, state_only=state_only)
    consts, const_specs = _split_consts((gm, wqkvg, wr, gw2, gb, hg, wout, s0, tri, gf, w1, w2c, fg))
    state_shape = (GLA_HEADS, GLA_HK, GLA_HV)
    in_map, out_map = _tile_maps(n_tiles, pipelined)
    return pl.pallas_call(
        kern,
        grid=(n_tiles + pipelined,),
        in_specs=[pl.BlockSpec((tile, d), in_map)] + const_specs,
        out_specs=[pl.BlockSpec((tile, d), out_map), _const_spec(state_shape)],
        out_shape=[jax.ShapeDtypeStruct(x2d.shape, F32), jax.ShapeDtypeStruct(state_shape, F32)],
        scratch_shapes=[pltpu.VMEM((tile, N_QKVG), F32),
                        pltpu.VMEM((tile, GLA_DK), F32),
                        pltpu.VMEM((tile, GLA_DK), F32),
                        pltpu.VMEM((tile // CHUNK, 8, GLA_DK), F32),
                        pltpu.VMEM((tile, GLA_DV), BF16),
                        pltpu.VMEM(state_shape, F32),
                        pltpu.VMEM((tile, d), F32)],
        compiler_params=pltpu.CompilerParams(dimension_semantics=("arbitrary",),
                                             vmem_limit_bytes=VMEM_LIMIT),
        name="layer1",
    )(x2d, *consts)


def kernel(x, meta_tokens, mix_norm_g, ffn_norm_g, ffn_w1, ffn_w2, cp_w_in, cp_conv_w, cp_conv_b,
           cp_ln_g, cp_ln_b, cp_pool_w, cp_pool_scale, cp_w_out, gla_w_in, gla_gate_w2, gla_gate_b,
           gla_head_g, gla_w_out, final_norm_g):
    bsz, seq, d = x.shape
    row = lambda v: v.reshape(1, -1).astype(F32)

    depth = ffn_w1.shape[0]
    w1 = _to_bf16(ffn_w1, (d, 2 * FF_CHUNK))
    w2c = _to_bf16(ffn_w2, (2 * FF_CHUNK, d)).reshape(depth, N_FF, FF_CHUNK, d)
    cp_win = _Layer(_to_bf16(cp_w_in, (d, FF_CHUNK)), 0)
    cp_wout = _Layer(_to_bf16(cp_w_out, (d, FF_CHUNK)), 0)
    pw_bd = jnp.zeros((POOL_DIM, POOL_DIM), F32)
    for gi in range(len(POOL_WINDOWS)):
        lo = gi * POOL_GROUP
        pw_bd = lax.dynamic_update_slice(pw_bd, cp_pool_w[0, gi], (lo, lo))
    pw_bd = pw_bd.astype(BF16)
    wqkvg = gla_w_in[0, :, :N_QKVG].astype(BF16)
    wr = jnp.pad(gla_w_in[0, :, N_QKVG:], ((0, 0), (0, LANES - GLA_GATE_RANK))).astype(BF16)
    gw2 = jnp.pad(gla_gate_w2[0], ((0, LANES - GLA_GATE_RANK), (0, 0))).astype(BF16)
    gla_wout = _Layer(_to_bf16(gla_w_out, (d, FF_CHUNK)), 0)
    tri = jnp.tril(jnp.ones((CHUNK, CHUNK), BF16))
    fg = row(final_norm_g)

    l0_args = (row(mix_norm_g[0]), cp_win, cp_conv_w[0].astype(F32), row(cp_conv_b[0]),
               row(cp_ln_g[0]), row(cp_ln_b[0]), pw_bd, row(cp_pool_scale[0]), cp_wout)
    l0_mlp = (row(ffn_norm_g[0]), _Layer(w1, 0), _Layer(w2c, 0))
    l1_args = (row(mix_norm_g[1]), wqkvg, wr, gw2, row(gla_gate_b[0]), row(gla_head_g[0]), gla_wout)
    l1_mlp = (row(ffn_norm_g[1]), _Layer(w1, 1), _Layer(w2c, 1), fg)

    hm, a_tail, p_tail = _layer0(meta_tokens.astype(F32), N_META, *l0_args,
                                 jnp.zeros((A_CARRY, CONV_DIM), F32),
                                 jnp.zeros((P_CARRY, POOL_DIM), F32), *l0_mlp, tile=N_META, l_off=0)
    hm = jnp.pad(hm, ((CHUNK - N_META, 0), (0, 0)))
    s_zero = jnp.zeros((GLA_HEADS, GLA_HK, GLA_HV), F32)
    _, s_meta = _layer1(hm, CHUNK, *l1_args, s_zero, tri, *l1_mlp, tile=CHUNK,
                        n_pad=CHUNK - N_META, final_norm=False, state_only=True)

    tile = 512
    h, _, _ = _layer0(x.astype(F32).reshape(bsz * seq, d), seq, *l0_args, a_tail, p_tail, *l0_mlp,
                      tile=tile, l_off=N_META)
    out, _ = _layer1(h, seq, *l1_args, s_meta, tri, *l1_mlp, tile=tile, n_pad=0, final_norm=True)
    return out.reshape(bsz, seq, d).astype(x.dtype)
```

```python
import functools

import jax
import jax.numpy as jnp
from jax import lax
from jax.experimental import pallas as pl
from jax.experimental.pallas import tpu as pltpu

D_MODEL = 1024
N_META = 16
CHUNK = 64
D_FF = 4 * D_MODEL
EPS = 1e-5

CONV_DIM = D_MODEL // 2
CONV_WIDTH = 31
POOL_DIM = D_MODEL // 2
POOL_WINDOWS = (2, 4, 8, 16)
POOL_GROUP = POOL_DIM // len(POOL_WINDOWS)

GLA_HEADS = 4
GLA_DK = D_MODEL // 2
GLA_DV = D_MODEL
GLA_HK = GLA_DK // GLA_HEADS
GLA_HV = GLA_DV // GLA_HEADS
GLA_GATE_RANK = 16
GLA_GATE_NORM = 16.0
N_QKVG = 2 * GLA_DK + 2 * GLA_DV

LANES = 128
N_SLAB = CONV_DIM // LANES
assert POOL_GROUP == LANES and POOL_DIM == CONV_DIM
A_CARRY = 32
P_CARRY = 16
ROW_BLOCK = 64
FF_CHUNK = 512
N_FF = D_FF // FF_CHUNK
VMEM_LIMIT = 56 * 1024 * 1024

F32 = jnp.float32
BF16 = jnp.bfloat16


def _rms(x, g):
    return x * lax.rsqrt(jnp.mean(x * x, axis=-1, keepdims=True) + EPS) * g


def _sigmoid(x):
    return 1.0 / (1.0 + jnp.exp(-x))


def _const_spec(shape):
    zeros = (0,) * len(shape)
    return pl.BlockSpec(shape, lambda *_: zeros)


class _Layer:
    def __init__(self, array, layer):
        self.array, self.layer = array, layer
        self.shape = array.shape[1:]

    def spec(self):
        index = (self.layer,) + (0,) * len(self.shape)
        return pl.BlockSpec((None,) + self.shape, lambda *_: index, pipeline_mode=pl.Buffered(1))


def _split_consts(consts):
    arrays = [c.array if isinstance(c, _Layer) else c for c in consts]
    specs = [c.spec() if isinstance(c, _Layer) else _const_spec(c.shape) for c in consts]
    return arrays, specs


def _cast_kernel(x_ref, o_ref):
    o_ref[...] = x_ref[...].astype(o_ref.dtype)


def _cast_split_kernel(x_ref, o_ref):
    for j in range(o_ref.shape[1]):
        o_ref[0, j] = x_ref[0, :, j * FF_CHUNK:(j + 1) * FF_CHUNK].astype(o_ref.dtype)


def _to_bf16_col_chunks(w):
    layers, rows, cols = w.shape
    per_step = 2
    n_chunks = cols // FF_CHUNK
    assert cols % (per_step * FF_CHUNK) == 0
    return pl.pallas_call(
        _cast_split_kernel,
        grid=(layers, n_chunks // per_step),
        in_specs=[pl.BlockSpec((1, rows, per_step * FF_CHUNK), lambda l, j: (l, 0, j))],
        out_specs=pl.BlockSpec((1, per_step, rows, FF_CHUNK), lambda l, j: (l, j, 0, 0)),
        out_shape=jax.ShapeDtypeStruct((layers, n_chunks, rows, FF_CHUNK), BF16),
        name="to_bf16_chunks",
    )(w)


def _to_bf16(w, block):
    layers, rows, cols = w.shape
    br, bc = block
    assert rows % br == 0 and cols % bc == 0
    spec = pl.BlockSpec((1, br, bc), lambda l, i, j: (l, i, j))
    return pl.pallas_call(
        _cast_kernel,
        grid=(layers, rows // br, cols // bc),
        in_specs=[spec],
        out_specs=spec,
        out_shape=jax.ShapeDtypeStruct((layers, rows, cols), BF16),
        name="to_bf16",
    )(w)


def _tile_maps(n_tiles, pipelined):
    if not pipelined:
        return (lambda s: (s, 0)), (lambda s: (s, 0))
    return (lambda s: (jnp.minimum(s, n_tiles - 1), 0)), (lambda s: (jnp.maximum(s - 1, 0), 0))


def _dependent_zero(v):
    bits = lax.bitcast_convert_type(v, jnp.uint32)
    bits = lax.shift_right_logical(lax.shift_right_logical(bits, jnp.uint32(16)), jnp.uint32(16))
    return lax.bitcast_convert_type(bits, F32)


def _delayed(token, hops=3):
    for _ in range(hops):
        s = jnp.sum(token, axis=-1, keepdims=True)
        token = _dependent_zero(jnp.broadcast_to(s, token.shape))
    return token


def _mlp_rolled(h1, ubuf, gf_ref, w1c_ref, w2c_ref, out_ref):
    out_ref[...] = h1
    ubuf[...] = _rms(h1, gf_ref[...]).astype(BF16)

    def chunk(c, carry):
        a = jnp.maximum(jnp.dot(ubuf[...], w1c_ref[c], preferred_element_type=F32), 0.0)
        out_ref[...] += jnp.dot((a * a).astype(BF16), w2c_ref[c], preferred_element_type=F32)
        return carry

    lax.fori_loop(0, N_FF, chunk, 0)


def _slabs(x):
    return jnp.stack([x[:, k * LANES:(k + 1) * LANES] for k in range(N_SLAB)])


def _unslab(x):
    return jnp.concatenate([x[k] for k in range(N_SLAB)], axis=-1)


def _conv_pool_block(r0, rb, pos0, abuf, pbuf, cbuf, mix, cw_ref, cb_ref, lng_ref, lnb_ref,
                     l_off):
    for k in range(N_SLAB):
        cols = slice(k * LANES, (k + 1) * LANES)
        acc = jnp.broadcast_to(cb_ref[:, cols], (rb, LANES))
        for j in range(CONV_WIDTH):
            off = A_CARRY - (CONV_WIDTH - 1) + j
            acc = acc + cw_ref[j:j + 1, cols] * abuf[k, pl.ds(r0 + off, rb), :]
        cbuf[k, pl.ds(r0, rb), :] = acc
    conv = [cbuf[k, pl.ds(r0, rb), :] for k in range(N_SLAB)]
    mu = sum(jnp.sum(c, axis=-1, keepdims=True) for c in conv) * (1.0 / CONV_DIM)
    cen = [c - mu for c in conv]
    var = sum(jnp.sum(c * c, axis=-1, keepdims=True) for c in cen) * (1.0 / CONV_DIM)
    rstd = lax.rsqrt(var + EPS)
    for k in range(N_SLAB):
        cols = slice(k * LANES, (k + 1) * LANES)
        y = cen[k] * rstd * lng_ref[:, cols] + lnb_ref[:, cols]
        mix[pl.ds(r0, rb), cols] = (y * _sigmoid(y)).astype(BF16)
    for k, w in enumerate(POOL_WINDOWS):
        win = pbuf[k, pl.ds(r0, P_CARRY + rb), :]
        cur = win[P_CARRY:]
        ws = win
        for step in range(w.bit_length() - 1):
            ws = ws + pltpu.roll(ws, 1 << step, axis=0)
        ws = ws[P_CARRY:]
        if l_off + 1 >= w:
            mean = ws * (1.0 / w)
        else:
            pos = pos0 + r0 + lax.broadcasted_iota(jnp.int32, (rb, LANES), 0)
            mean = ws / jnp.minimum(pos + 1, w).astype(F32)
        mix[pl.ds(r0, rb), CONV_DIM + k * LANES:CONV_DIM + (k + 1) * LANES] = (mean - cur).astype(BF16)


def _layer0_kernel(x_ref, gm_ref, win_ref, cw_ref, cb_ref, lng_ref, lnb_ref, pw_ref, ps_ref, wout_ref,
                   a0_ref, p0_ref, gf_ref, w1c_ref, w2c_ref, out_ref, atail_ref, ptail_ref,
                   abuf, pbuf, cbuf, mix, hbuf, *, tile, l_off, n_tiles, tiles_per_seq, pipelined):
    s = pl.program_id(0)
    t_in_seq = lax.rem(s, tiles_per_seq)
    rb = min(ROW_BLOCK, tile)
    pos0 = l_off + t_in_seq * tile

    def conv_blocks():
        for i in range(tile // rb):
            _conv_pool_block(i * rb, rb, pos0, abuf, pbuf, cbuf, mix, cw_ref, cb_ref, lng_ref,
                             lnb_ref, l_off)

    @pl.when(t_in_seq == 0)
    def _():
        abuf[:, 0:A_CARRY, :] = _slabs(a0_ref[...])
        pbuf[:, 0:P_CARRY, :] = _slabs(p0_ref[...])

    def input_projection():
        u = _rms(x_ref[...], gm_ref[...]).astype(BF16)
        z = jnp.dot(u, win_ref[...], preferred_element_type=F32)
        return z[:, :CONV_DIM] * _sigmoid(z[:, CONV_DIM:2 * CONV_DIM]), z[:, 2 * CONV_DIM:]

    def output_projection():
        pooled = jnp.dot(mix[:, CONV_DIM:], pw_ref[...], preferred_element_type=F32) * ps_ref[...]
        y = jnp.dot(mix[:, 0:CONV_DIM], wout_ref[0:CONV_DIM, :], preferred_element_type=F32)
        return y + jnp.dot(pooled.astype(BF16), wout_ref[CONV_DIM:, :], preferred_element_type=F32)

    def mlp(h1):
        uu = _rms(h1, gf_ref[...]).astype(BF16)
        acc = h1
        for c in range(N_FF):
            a = jnp.maximum(jnp.dot(uu, w1c_ref[c],
                                    preferred_element_type=F32), 0.0)
            acc = acc + jnp.dot((a * a).astype(BF16), w2c_ref[c], preferred_element_type=F32)
        out_ref[...] = acc

    def finish_front():
        a_tail = abuf[:, tile:tile + A_CARRY, :]
        p_tail = pbuf[:, tile:tile + P_CARRY, :]
        abuf[:, 0:A_CARRY, :] = a_tail
        pbuf[:, 0:P_CARRY, :] = p_tail

        @pl.when(s == n_tiles - 1)
        def _():
            atail_ref[...] = _unslab(a_tail)
            ptail_ref[...] = _unslab(p_tail)

    def front_only():
        glu, pin = input_projection()
        abuf[:, A_CARRY:A_CARRY + tile, :] = _slabs(glu)
        pbuf[:, P_CARRY:P_CARRY + tile, :] = _slabs(pin)
        conv_blocks()

    def front_and_back():
        y = output_projection()
        glu, pin = input_projection()
        h1 = hbuf[...] + y
        token = _dependent_zero(glu[0:8, 0:LANES])
        for i in range(tile // rb):
            rows = slice(i * rb, (i + 1) * rb)
            for k in range(N_SLAB):
                zero = jnp.tile(token, (rb // 8, 1))
                cols = slice(k * LANES, (k + 1) * LANES)
                abuf[k, A_CARRY + i * rb:A_CARRY + (i + 1) * rb, :] = glu[rows, cols] + zero
                pbuf[k, P_CARRY + i * rb:P_CARRY + (i + 1) * rb, :] = pin[rows, cols] + zero
                token = _delayed(token)
        mlp(h1)
        conv_blocks()

    if not pipelined:
        front_only()
        mlp(x_ref[...] + output_projection())
        finish_front()
    else:
        @pl.when(s == 0)
        def _():
            front_only()
            hbuf[...] = x_ref[...]
            finish_front()

        @pl.when(jnp.logical_and(s > 0, s < n_tiles))
        def _():
            front_and_back()
            hbuf[...] = x_ref[...]
            finish_front()

        @pl.when(s == n_tiles)
        def _():
            h1 = hbuf[...] + output_projection()
            _mlp_rolled(h1, mix, gf_ref, w1c_ref, w2c_ref, out_ref)


def _layer0(x2d, seq, gm, win, cw, cb, lng, lnb, pw_bd, ps, wout, a0, p0, gf, w1c, w2c, *, tile,
            l_off):
    n, d = x2d.shape
    assert seq % tile == 0 and n % seq == 0 and tile % min(ROW_BLOCK, tile) == 0
    n_tiles = n // tile
    pipelined = n_tiles > 1
    kern = functools.partial(_layer0_kernel, tile=tile, l_off=l_off, n_tiles=n_tiles,
                             tiles_per_seq=seq // tile, pipelined=pipelined)
    consts, const_specs = _split_consts((gm, win, cw, cb, lng, lnb, pw_bd, ps, wout, a0, p0, gf, w1c,
                                         w2c))
    in_map, out_map = _tile_maps(n_tiles, pipelined)
    return pl.pallas_call(
        kern,
        grid=(n_tiles + pipelined,),
        in_specs=[pl.BlockSpec((tile, d), in_map)] + const_specs,
        out_specs=[pl.BlockSpec((tile, d), out_map),
                   _const_spec((A_CARRY, CONV_DIM)), _const_spec((P_CARRY, POOL_DIM))],
        out_shape=[jax.ShapeDtypeStruct(x2d.shape, F32),
                   jax.ShapeDtypeStruct((A_CARRY, CONV_DIM), F32),
                   jax.ShapeDtypeStruct((P_CARRY, POOL_DIM), F32)],
        scratch_shapes=[pltpu.VMEM((N_SLAB, A_CARRY + tile, LANES), F32),
                        pltpu.VMEM((N_SLAB, P_CARRY + tile, LANES), F32),
                        pltpu.VMEM((N_SLAB, tile, LANES), F32),
                        pltpu.VMEM((tile, CONV_DIM + POOL_DIM), BF16),
                        pltpu.VMEM((tile, d), F32)],
        compiler_params=pltpu.CompilerParams(dimension_semantics=("arbitrary",),
                                             vmem_limit_bytes=VMEM_LIMIT),
        name="layer0",
    )(x2d, *consts)


def _chunk_cumsum(tri_ref, x):
    tri = tri_ref[...]
    total = None
    for _ in range(3):
        part = x.astype(BF16)
        x = x - part.astype(F32)
        term = jnp.dot(tri, part, preferred_element_type=F32)
        total = term if total is None else total + term
    return total


def _gla_decay(la_scr, dec_scr, etot_scr, tri_ref, n_chunks):
    for c in range(n_chunks):
        rows = slice(c * CHUNK, (c + 1) * CHUNK)
        cum = _chunk_cumsum(tri_ref, la_scr[rows, :])
        tot = cum[CHUNK - 1:CHUNK, :]
        dec_scr[rows, :] = jnp.exp(tot - cum)
        etot_scr[c] = jnp.broadcast_to(jnp.exp(tot), (8, GLA_DK))


def _gla_update(c, r0, z_scr, dec_scr, etot_scr, state):
    rows = pl.ds(r0, CHUNK)
    kd = (z_scr[rows, GLA_DK:2 * GLA_DK] * dec_scr[rows, :]).astype(BF16)
    etot = etot_scr[c][0:1, :]
    for hh in range(GLA_HEADS):
        klo, vlo = hh * GLA_HK, 2 * GLA_DK + hh * GLA_HV
        v_h = z_scr[rows, vlo:vlo + GLA_HV].astype(BF16)
        upd = lax.dot_general(kd[:, klo:klo + GLA_HK], v_h, (((0,), (0,)), ((), ())),
                              preferred_element_type=F32)
        ecol = jnp.transpose(jnp.broadcast_to(etot[:, klo:klo + GLA_HK], (GLA_HK, GLA_HK)))
        state[hh] = state[hh] * jnp.tile(ecol, (1, GLA_HV // GLA_HK)) + upd


def _gla_output(r0, z_scr, og, state, hg_ref):
    rows = pl.ds(r0, CHUNK)
    q = (z_scr[rows, 0:GLA_DK] * (GLA_HK ** -0.5)).astype(BF16)
    glo = 2 * GLA_DK + GLA_DV
    for hh in range(GLA_HEADS):
        klo = hh * GLA_HK
        o = jnp.dot(q[:, klo:klo + GLA_HK], state[hh].astype(BF16),
                    preferred_element_type=F32)
        o = o * lax.rsqrt(jnp.mean(o * o, axis=-1, keepdims=True) + EPS) * hg_ref[...]
        gt = z_scr[rows, glo + hh * GLA_HV:glo + (hh + 1) * GLA_HV]
        og[rows, hh * GLA_HV:(hh + 1) * GLA_HV] = (o * (gt * _sigmoid(gt))).astype(BF16)


def _layer1_kernel(x_ref, gm_ref, wqkvg_ref, wr_ref, gw2_ref, gb_ref, hg_ref, wout_ref, s0_ref,
                   tri_ref, gf_ref, w1c_ref, w2c_ref, fg_ref, out_ref, sfin_ref,
                   z_scr, la_scr, dec_scr, etot_scr, og, state, hbuf, *, tile, n_pad, n_tiles,
                   tiles_per_seq, final_norm, state_only):
    s = pl.program_id(0)
    t_in_seq = lax.rem(s, tiles_per_seq)
    n_chunks = tile // CHUNK

    @pl.when(t_in_seq == 0)
    def _():
        state[...] = s0_ref[...]

    def input_projection():
        u = _rms(x_ref[...], gm_ref[...]).astype(BF16)
        r = jnp.dot(u, wr_ref[...], preferred_element_type=F32)
        x = jnp.dot(r.astype(BF16), gw2_ref[...], preferred_element_type=F32) + gb_ref[...]
        la = (jnp.minimum(x, 0.0) - jnp.log1p(jnp.exp(-jnp.abs(x)))) * (1.0 / GLA_GATE_NORM)
        if n_pad:
            row = t_in_seq * tile + lax.broadcasted_iota(jnp.int32, (tile, GLA_DK), 0)
            la = jnp.where(row >= n_pad, la, 0.0)
        la_scr[...] = la
        z_scr[...] = jnp.dot(u, wqkvg_ref[...], preferred_element_type=F32)
        _gla_decay(la_scr, dec_scr, etot_scr, tri_ref, n_chunks)

    def output_projection():
        return hbuf[...] + jnp.dot(og[...], wout_ref[...], preferred_element_type=F32)

    def mlp_and_gla(h1):
        uu = _rms(h1, gf_ref[...]).astype(BF16)
        acc = h1
        gla_chunks = n_chunks
        for c in range(max(N_FF, gla_chunks)):
            if c < N_FF:
                a = jnp.maximum(jnp.dot(uu, w1c_ref[c],
                                        preferred_element_type=F32), 0.0)
            if c < gla_chunks:
                _gla_update(c, c * CHUNK, z_scr, dec_scr, etot_scr, state)
            if c < N_FF:
                acc = acc + jnp.dot((a * a).astype(BF16), w2c_ref[c], preferred_element_type=F32)
            if c < gla_chunks:
                _gla_output(c * CHUNK, z_scr, og, state, hg_ref)
        if final_norm:
            acc = _rms(acc, fg_ref[...])
        out_ref[...] = acc

    if state_only:
        input_projection()
        for c in range(n_chunks):
            _gla_update(c, c * CHUNK, z_scr, dec_scr, etot_scr, state)
        out_ref[...] = x_ref[...]
    else:
        @pl.when(s == 0)
        def _():
            input_projection()

            def chunk(c, carry):
                r0 = pl.multiple_of(c * CHUNK, CHUNK)
                _gla_update(c, r0, z_scr, dec_scr, etot_scr, state)
                _gla_output(r0, z_scr, og, state, hg_ref)
                return carry

            lax.fori_loop(0, n_chunks, chunk, 0)
            hbuf[...] = x_ref[...]

        @pl.when(jnp.logical_and(s > 0, s < n_tiles))
        def _():
            h1 = output_projection()
            input_projection()
            mlp_and_gla(h1)
            hbuf[...] = x_ref[...]

        @pl.when(s == n_tiles)
        def _():
            h1 = output_projection()
            _mlp_rolled(h1, og, gf_ref, w1c_ref, w2c_ref, out_ref)
            if final_norm:
                out_ref[...] = _rms(out_ref[...], fg_ref[...])

    @pl.when(s == n_tiles - 1)
    def _():
        sfin_ref[...] = state[...]


def _layer1(x2d, seq, gm, wqkvg, wr, gw2, gb, hg, wout, s0, tri, gf, w1c, w2c, fg, *, tile, n_pad,
            final_norm, state_only=False):
    n, d = x2d.shape
    assert seq % tile == 0 and n % seq == 0 and tile % CHUNK == 0
    n_tiles = n // tile
    pipelined = not state_only
    if state_only:
        wout = w1c = w2c = jnp.zeros((8, LANES), BF16)
    kern = functools.partial(_layer1_kernel, tile=tile, n_pad=n_pad, n_tiles=n_tiles,
                             tiles_per_seq=seq // tile, final_norm=final_norm, state_only=state_only)
    consts, const_specs = _split_consts((gm, wqkvg, wr, gw2, gb, hg, wout, s0, tri, gf, w1c, w2c, fg))
    state_shape = (GLA_HEADS, GLA_HK, GLA_HV)
    in_map, out_map = _tile_maps(n_tiles, pipelined)
    return pl.pallas_call(
        kern,
        grid=(n_tiles + pipelined,),
        in_specs=[pl.BlockSpec((tile, d), in_map)] + const_specs,
        out_specs=[pl.BlockSpec((tile, d), out_map), _const_spec(state_shape)],
        out_shape=[jax.ShapeDtypeStruct(x2d.shape, F32), jax.ShapeDtypeStruct(state_shape, F32)],
        scratch_shapes=[pltpu.VMEM((tile, N_QKVG), F32),
                        pltpu.VMEM((tile, GLA_DK), F32),
                        pltpu.VMEM((tile, GLA_DK), F32),
                        pltpu.VMEM((tile // CHUNK, 8, GLA_DK), F32),
                        pltpu.VMEM((tile, GLA_DV), BF16),
                        pltpu.VMEM(state_shape, F32),
                        pltpu.VMEM((tile, d), F32)],
        compiler_params=pltpu.CompilerParams(dimension_semantics=("arbitrary",),
                                             vmem_limit_bytes=VMEM_LIMIT),
        name="layer1",
    )(x2d, *consts)


def kernel(x, meta_tokens, mix_norm_g, ffn_norm_g, ffn_w1, ffn_w2, cp_w_in, cp_conv_w, cp_conv_b,
           cp_ln_g, cp_ln_b, cp_pool_w, cp_pool_scale, cp_w_out, gla_w_in, gla_gate_w2, gla_gate_b,
           gla_head_g, gla_w_out, final_norm_g):
    bsz, seq, d = x.shape
    row = lambda v: v.reshape(1, -1).astype(F32)

    depth = ffn_w1.shape[0]
    w1c = _to_bf16_col_chunks(ffn_w1)
    w2c = _to_bf16(ffn_w2, (2 * FF_CHUNK, d)).reshape(depth, N_FF, FF_CHUNK, d)
    cp_win = _Layer(_to_bf16(cp_w_in, (d, FF_CHUNK)), 0)
    cp_wout = _Layer(_to_bf16(cp_w_out, (d, FF_CHUNK)), 0)
    pw_bd = jnp.zeros((POOL_DIM, POOL_DIM), F32)
    for gi in range(len(POOL_WINDOWS)):
        lo = gi * POOL_GROUP
        pw_bd = lax.dynamic_update_slice(pw_bd, cp_pool_w[0, gi], (lo, lo))
    pw_bd = pw_bd.astype(BF16)
    wqkvg = gla_w_in[0, :, :N_QKVG].astype(BF16)
    wr = jnp.pad(gla_w_in[0, :, N_QKVG:], ((0, 0), (0, LANES - GLA_GATE_RANK))).astype(BF16)
    gw2 = jnp.pad(gla_gate_w2[0], ((0, LANES - GLA_GATE_RANK), (0, 0))).astype(BF16)
    gla_wout = _Layer(_to_bf16(gla_w_out, (d, FF_CHUNK)), 0)
    tri = jnp.tril(jnp.ones((CHUNK, CHUNK), BF16))
    fg = row(final_norm_g)

    l0_args = (row(mix_norm_g[0]), cp_win, cp_conv_w[0].astype(F32), row(cp_conv_b[0]),
               row(cp_ln_g[0]), row(cp_ln_b[0]), pw_bd, row(cp_pool_scale[0]), cp_wout)
    l0_mlp = (row(ffn_norm_g[0]), _Layer(w1c, 0), _Layer(w2c, 0))
    l1_args = (row(mix_norm_g[1]), wqkvg, wr, gw2, row(gla_gate_b[0]), row(gla_head_g[0]), gla_wout)
    l1_mlp = (row(ffn_norm_g[1]), _Layer(w1c, 1), _Layer(w2c, 1), fg)

    hm, a_tail, p_tail = _layer0(meta_tokens.astype(F32), N_META, *l0_args,
                                 jnp.zeros((A_CARRY, CONV_DIM), F32),
                                 jnp.zeros((P_CARRY, POOL_DIM), F32), *l0_mlp, tile=N_META, l_off=0)
    hm = jnp.pad(hm, ((CHUNK - N_META, 0), (0, 0)))
    s_zero = jnp.zeros((GLA_HEADS, GLA_HK, GLA_HV), F32)
    _, s_meta = _layer1(hm, CHUNK, *l1_args, s_zero, tri, *l1_mlp, tile=CHUNK,
                        n_pad=CHUNK - N_META, final_norm=False, state_only=True)

    tile = 512
    h, _, _ = _layer0(x.astype(F32).reshape(bsz * seq, d), seq, *l0_args, a_tail, p_tail, *l0_mlp,
                      tile=tile, l_off=N_META)
    out, _ = _layer1(h, seq, *l1_args, s_meta, tri, *l1_mlp, tile=tile, n_pad=0, final_norm=True)
    return out.reshape(bsz, seq, d).astype(x.dtype)
```

```python
import functools

import jax
import jax.numpy as jnp
from jax import lax
from jax.experimental import pallas as pl
from jax.experimental.pallas import tpu as pltpu

D_MODEL = 1024
N_META = 16
CHUNK = 64
D_FF = 4 * D_MODEL
EPS = 1e-5

CONV_DIM = D_MODEL // 2
CONV_WIDTH = 31
POOL_DIM = D_MODEL // 2
POOL_WINDOWS = (2, 4, 8, 16)
POOL_GROUP = POOL_DIM // len(POOL_WINDOWS)

GLA_HEADS = 4
GLA_DK = D_MODEL // 2
GLA_DV = D_MODEL
GLA_HK = GLA_DK // GLA_HEADS
GLA_HV = GLA_DV // GLA_HEADS
GLA_GATE_RANK = 16
GLA_GATE_NORM = 16.0
N_QKVG = 2 * GLA_DK + 2 * GLA_DV

LANES = 128
N_SLAB = CONV_DIM // LANES
assert POOL_GROUP == LANES and POOL_DIM == CONV_DIM
A_CARRY = 32
P_CARRY = 16
ROW_BLOCK = 64
FF_CHUNK = 512
N_FF = D_FF // FF_CHUNK
VMEM_LIMIT = 56 * 1024 * 1024

F32 = jnp.float32
BF16 = jnp.bfloat16


def _rms(x, g):
    return x * lax.rsqrt(jnp.mean(x * x, axis=-1, keepdims=True) + EPS) * g


def _sigmoid(x):
    return 1.0 / (1.0 + jnp.exp(-x))


def _const_spec(shape):
    zeros = (0,) * len(shape)
    return pl.BlockSpec(shape, lambda *_: zeros)


class _Layer:
    def __init__(self, array, layer):
        self.array, self.layer = array, layer
        self.shape = array.shape[1:]

    def spec(self):
        index = (self.layer,) + (0,) * len(self.shape)
        return pl.BlockSpec((None,) + self.shape, lambda *_: index, pipeline_mode=pl.Buffered(1))


def _split_consts(consts):
    arrays = [c.array if isinstance(c, _Layer) else c for c in consts]
    specs = [c.spec() if isinstance(c, _Layer) else _const_spec(c.shape) for c in consts]
    return arrays, specs


def _late_operands(*layers):
    arrays = [w.array for w in layers]
    specs = [pl.BlockSpec(memory_space=pl.ANY) for _ in layers]
    scratch = [pltpu.VMEM(w.shape, w.array.dtype) for w in layers]
    return arrays, specs, scratch + [pltpu.SemaphoreType.DMA((len(layers),))]


def _cast_kernel(x_ref, o_ref):
    o_ref[...] = x_ref[...].astype(o_ref.dtype)


def _cast_split_kernel(x_ref, o_ref):
    for j in range(o_ref.shape[1]):
        o_ref[0, j] = x_ref[0, :, j * FF_CHUNK:(j + 1) * FF_CHUNK].astype(o_ref.dtype)


def _to_bf16_col_chunks(w):
    layers, rows, cols = w.shape
    per_step = 2
    n_chunks = cols // FF_CHUNK
    assert cols % (per_step * FF_CHUNK) == 0
    return pl.pallas_call(
        _cast_split_kernel,
        grid=(layers, n_chunks // per_step),
        in_specs=[pl.BlockSpec((1, rows, per_step * FF_CHUNK), lambda l, j: (l, 0, j))],
        out_specs=pl.BlockSpec((1, per_step, rows, FF_CHUNK), lambda l, j: (l, j, 0, 0)),
        out_shape=jax.ShapeDtypeStruct((layers, n_chunks, rows, FF_CHUNK), BF16),
        name="to_bf16_chunks",
    )(w)


def _to_bf16(w, block):
    layers, rows, cols = w.shape
    br, bc = block
    assert rows % br == 0 and cols % bc == 0
    spec = pl.BlockSpec((1, br, bc), lambda l, i, j: (l, i, j))
    return pl.pallas_call(
        _cast_kernel,
        grid=(layers, rows // br, cols // bc),
        in_specs=[spec],
        out_specs=spec,
        out_shape=jax.ShapeDtypeStruct((layers, rows, cols), BF16),
        name="to_bf16",
    )(w)


def _tile_maps(n_tiles, pipelined):
    if not pipelined:
        return (lambda s: (s, 0)), (lambda s: (s, 0))
    return (lambda s: (jnp.minimum(s, n_tiles - 1), 0)), (lambda s: (jnp.maximum(s - 1, 0), 0))


def _dependent_zero(v):
    bits = lax.bitcast_convert_type(v, jnp.uint32)
    bits = lax.shift_right_logical(lax.shift_right_logical(bits, jnp.uint32(16)), jnp.uint32(16))
    return lax.bitcast_convert_type(bits, F32)


def _delayed(token, hops=3):
    for _ in range(hops):
        s = jnp.sum(token, axis=-1, keepdims=True)
        token = _dependent_zero(jnp.broadcast_to(s, token.shape))
    return token


def _late_weight_copies(layer, w1c_hbm, w2c_hbm, w1c_ref, w2c_ref, sem):
    return (pltpu.make_async_copy(w1c_hbm.at[layer], w1c_ref, sem.at[0]),
            pltpu.make_async_copy(w2c_hbm.at[layer], w2c_ref, sem.at[1]))


def _slabs(x):
    return jnp.stack([x[:, k * LANES:(k + 1) * LANES] for k in range(N_SLAB)])


def _unslab(x):
    return jnp.concatenate([x[k] for k in range(N_SLAB)], axis=-1)


def _conv_pool_block(r0, rb, pos0, abuf, pbuf, cbuf, mix, cw_ref, cb_ref, lng_ref, lnb_ref,
                     l_off):
    for k in range(N_SLAB):
        cols = slice(k * LANES, (k + 1) * LANES)
        acc = jnp.broadcast_to(cb_ref[:, cols], (rb, LANES))
        for j in range(CONV_WIDTH):
            off = A_CARRY - (CONV_WIDTH - 1) + j
            acc = acc + cw_ref[j:j + 1, cols] * abuf[k, pl.ds(r0 + off, rb), :]
        cbuf[k, pl.ds(r0, rb), :] = acc
    conv = [cbuf[k, pl.ds(r0, rb), :] for k in range(N_SLAB)]
    mu = sum(jnp.sum(c, axis=-1, keepdims=True) for c in conv) * (1.0 / CONV_DIM)
    cen = [c - mu for c in conv]
    var = sum(jnp.sum(c * c, axis=-1, keepdims=True) for c in cen) * (1.0 / CONV_DIM)
    rstd = lax.rsqrt(var + EPS)
    for k in range(N_SLAB):
        cols = slice(k * LANES, (k + 1) * LANES)
        y = cen[k] * rstd * lng_ref[:, cols] + lnb_ref[:, cols]
        mix[pl.ds(r0, rb), cols] = (y * _sigmoid(y)).astype(BF16)
    for k, w in enumerate(POOL_WINDOWS):
        win = pbuf[k, pl.ds(r0, P_CARRY + rb), :]
        cur = win[P_CARRY:]
        ws = win
        for step in range(w.bit_length() - 1):
            ws = ws + pltpu.roll(ws, 1 << step, axis=0)
        ws = ws[P_CARRY:]
        if l_off + 1 >= w:
            mean = ws * (1.0 / w)
        else:
            pos = pos0 + r0 + lax.broadcasted_iota(jnp.int32, (rb, LANES), 0)
            mean = ws / jnp.minimum(pos + 1, w).astype(F32)
        mix[pl.ds(r0, rb), CONV_DIM + k * LANES:CONV_DIM + (k + 1) * LANES] = (mean - cur).astype(BF16)


def _layer0_kernel(x_ref, gm_ref, win_ref, cw_ref, cb_ref, lng_ref, lnb_ref, pw_ref, ps_ref, wout_ref,
                   a0_ref, p0_ref, gf_ref, w1c_hbm, w2c_hbm, out_ref, atail_ref, ptail_ref,
                   abuf, pbuf, cbuf, mix, hbuf, w1c_ref, w2c_ref, sem, *, tile, l_off, n_tiles,
                   tiles_per_seq, pipelined, layer):
    s = pl.program_id(0)
    t_in_seq = lax.rem(s, tiles_per_seq)
    rb = min(ROW_BLOCK, tile)
    pos0 = l_off + t_in_seq * tile
    mlp_weights = _late_weight_copies(layer, w1c_hbm, w2c_hbm, w1c_ref, w2c_ref, sem)

    @pl.when(s == 0)
    def _():
        for cp in mlp_weights:
            cp.start()

    def conv_blocks():
        for i in range(tile // rb):
            _conv_pool_block(i * rb, rb, pos0, abuf, pbuf, cbuf, mix, cw_ref, cb_ref, lng_ref,
                             lnb_ref, l_off)

    @pl.when(t_in_seq == 0)
    def _():
        abuf[:, 0:A_CARRY, :] = _slabs(a0_ref[...])
        pbuf[:, 0:P_CARRY, :] = _slabs(p0_ref[...])

    def input_projection():
        u = _rms(x_ref[...], gm_ref[...]).astype(BF16)
        z = jnp.dot(u, win_ref[...], preferred_element_type=F32)
        return z[:, :CONV_DIM] * _sigmoid(z[:, CONV_DIM:2 * CONV_DIM]), z[:, 2 * CONV_DIM:]

    def output_projection():
        pooled = jnp.dot(mix[:, CONV_DIM:], pw_ref[...], preferred_element_type=F32) * ps_ref[...]
        y = jnp.dot(mix[:, 0:CONV_DIM], wout_ref[0:CONV_DIM, :], preferred_element_type=F32)
        return y + jnp.dot(pooled.astype(BF16), wout_ref[CONV_DIM:, :], preferred_element_type=F32)

    def mlp(h1):
        uu = _rms(h1, gf_ref[...]).astype(BF16)
        acc = h1
        for c in range(N_FF):
            a = jnp.maximum(jnp.dot(uu, w1c_ref[c], preferred_element_type=F32), 0.0)
            acc = acc + jnp.dot((a * a).astype(BF16), w2c_ref[c], preferred_element_type=F32)
        out_ref[...] = acc

    def finish_front():
        a_tail = abuf[:, tile:tile + A_CARRY, :]
        p_tail = pbuf[:, tile:tile + P_CARRY, :]
        abuf[:, 0:A_CARRY, :] = a_tail
        pbuf[:, 0:P_CARRY, :] = p_tail

        @pl.when(s == n_tiles - 1)
        def _():
            atail_ref[...] = _unslab(a_tail)
            ptail_ref[...] = _unslab(p_tail)

    def front_only():
        glu, pin = input_projection()
        abuf[:, A_CARRY:A_CARRY + tile, :] = _slabs(glu)
        pbuf[:, P_CARRY:P_CARRY + tile, :] = _slabs(pin)
        conv_blocks()

    def front_and_back():
        y = output_projection()
        glu, pin = input_projection()
        h1 = hbuf[...] + y
        token = _dependent_zero(glu[0:8, 0:LANES])
        for i in range(tile // rb):
            rows = slice(i * rb, (i + 1) * rb)
            for k in range(N_SLAB):
                zero = jnp.tile(token, (rb // 8, 1))
                cols = slice(k * LANES, (k + 1) * LANES)
                abuf[k, A_CARRY + i * rb:A_CARRY + (i + 1) * rb, :] = glu[rows, cols] + zero
                pbuf[k, P_CARRY + i * rb:P_CARRY + (i + 1) * rb, :] = pin[rows, cols] + zero
                token = _delayed(token)
        mlp(h1)
        conv_blocks()

    if not pipelined:
        front_only()
        for cp in mlp_weights:
            cp.wait()
        mlp(x_ref[...] + output_projection())
        finish_front()
    else:
        @pl.when(s == 0)
        def _():
            front_only()
            hbuf[...] = x_ref[...]
            finish_front()

        @pl.when(s == 1)
        def _():
            for cp in mlp_weights:
                cp.wait()

        @pl.when(jnp.logical_and(s > 0, s < n_tiles))
        def _():
            front_and_back()
            hbuf[...] = x_ref[...]
            finish_front()

        @pl.when(s == n_tiles)
        def _():
            mlp(hbuf[...] + output_projection())


def _layer0(x2d, seq, gm, win, cw, cb, lng, lnb, pw_bd, ps, wout, a0, p0, gf, w1c, w2c, *, tile,
            l_off):
    n, d = x2d.shape
    assert seq % tile == 0 and n % seq == 0 and tile % min(ROW_BLOCK, tile) == 0
    n_tiles = n // tile
    pipelined = n_tiles > 1
    kern = functools.partial(_layer0_kernel, tile=tile, l_off=l_off, n_tiles=n_tiles,
                             tiles_per_seq=seq // tile, pipelined=pipelined, layer=w1c.layer)
    consts, const_specs = _split_consts((gm, win, cw, cb, lng, lnb, pw_bd, ps, wout, a0, p0, gf))
    late, late_specs, late_scratch = _late_operands(w1c, w2c)
    in_map, out_map = _tile_maps(n_tiles, pipelined)
    return pl.pallas_call(
        kern,
        grid=(n_tiles + pipelined,),
        in_specs=[pl.BlockSpec((tile, d), in_map)] + const_specs + late_specs,
        out_specs=[pl.BlockSpec((tile, d), out_map),
                   _const_spec((A_CARRY, CONV_DIM)), _const_spec((P_CARRY, POOL_DIM))],
        out_shape=[jax.ShapeDtypeStruct(x2d.shape, F32),
                   jax.ShapeDtypeStruct((A_CARRY, CONV_DIM), F32),
                   jax.ShapeDtypeStruct((P_CARRY, POOL_DIM), F32)],
        scratch_shapes=[pltpu.VMEM((N_SLAB, A_CARRY + tile, LANES), F32),
                        pltpu.VMEM((N_SLAB, P_CARRY + tile, LANES), F32),
                        pltpu.VMEM((N_SLAB, tile, LANES), F32),
                        pltpu.VMEM((tile, CONV_DIM + POOL_DIM), BF16),
                        pltpu.VMEM((tile, d), F32)] + late_scratch,
        compiler_params=pltpu.CompilerParams(dimension_semantics=("arbitrary",),
                                             vmem_limit_bytes=VMEM_LIMIT),
        name="layer0",
    )(x2d, *consts, *late)


def _chunk_cumsum(tri_ref, x):
    tri = tri_ref[...]
    total = None
    for _ in range(3):
        part = x.astype(BF16)
        x = x - part.astype(F32)
        term = jnp.dot(tri, part, preferred_element_type=F32)
        total = term if total is None else total + term
    return total


def _gla_decay(la_scr, dec_scr, etot_scr, tri_ref, n_chunks):
    for c in range(n_chunks):
        rows = slice(c * CHUNK, (c + 1) * CHUNK)
        cum = _chunk_cumsum(tri_ref, la_scr[rows, :])
        tot = cum[CHUNK - 1:CHUNK, :]
        dec_scr[rows, :] = jnp.exp(tot - cum)
        etot_scr[c] = jnp.broadcast_to(jnp.exp(tot), (8, GLA_DK))


def _gla_update(c, r0, z_scr, dec_scr, etot_scr, state):
    rows = pl.ds(r0, CHUNK)
    kd = (z_scr[rows, GLA_DK:2 * GLA_DK] * dec_scr[rows, :]).astype(BF16)
    etot = etot_scr[c][0:1, :]
    for hh in range(GLA_HEADS):
        klo, vlo = hh * GLA_HK, 2 * GLA_DK + hh * GLA_HV
        v_h = z_scr[rows, vlo:vlo + GLA_HV].astype(BF16)
        upd = lax.dot_general(kd[:, klo:klo + GLA_HK], v_h, (((0,), (0,)), ((), ())),
                              preferred_element_type=F32)
        ecol = jnp.transpose(jnp.broadcast_to(etot[:, klo:klo + GLA_HK], (GLA_HK, GLA_HK)))
        state[hh] = state[hh] * jnp.tile(ecol, (1, GLA_HV // GLA_HK)) + upd


def _gla_output(r0, z_scr, og, state, hg_ref):
    rows = pl.ds(r0, CHUNK)
    q = (z_scr[rows, 0:GLA_DK] * (GLA_HK ** -0.5)).astype(BF16)
    glo = 2 * GLA_DK + GLA_DV
    for hh in range(GLA_HEADS):
        klo = hh * GLA_HK
        o = jnp.dot(q[:, klo:klo + GLA_HK], state[hh].astype(BF16),
                    preferred_element_type=F32)
        o = o * lax.rsqrt(jnp.mean(o * o, axis=-1, keepdims=True) + EPS) * hg_ref[...]
        gt = z_scr[rows, glo + hh * GLA_HV:glo + (hh + 1) * GLA_HV]
        og[rows, hh * GLA_HV:(hh + 1) * GLA_HV] = (o * (gt * _sigmoid(gt))).astype(BF16)


def _layer1_kernel(x_ref, gm_ref, wqkvg_ref, wr_ref, gw2_ref, gb_ref, hg_ref, wout_ref, s0_ref,
                   tri_ref, gf_ref, fg_ref, w1c_hbm, w2c_hbm, out_ref, sfin_ref,
                   z_scr, la_scr, dec_scr, etot_scr, og, state, hbuf, w1c_ref, w2c_ref, sem, *, tile,
                   n_pad, n_tiles, tiles_per_seq, final_norm, state_only, layer):
    s = pl.program_id(0)
    t_in_seq = lax.rem(s, tiles_per_seq)
    n_chunks = tile // CHUNK
    if not state_only:
        mlp_weights = _late_weight_copies(layer, w1c_hbm, w2c_hbm, w1c_ref, w2c_ref, sem)

        @pl.when(s == 0)
        def _():
            for cp in mlp_weights:
                cp.start()

        @pl.when(s == 1)
        def _():
            for cp in mlp_weights:
                cp.wait()

    @pl.when(t_in_seq == 0)
    def _():
        state[...] = s0_ref[...]

    def input_projection():
        u = _rms(x_ref[...], gm_ref[...]).astype(BF16)
        r = jnp.dot(u, wr_ref[...], preferred_element_type=F32)
        x = jnp.dot(r.astype(BF16), gw2_ref[...], preferred_element_type=F32) + gb_ref[...]
        la = (jnp.minimum(x, 0.0) - jnp.log1p(jnp.exp(-jnp.abs(x)))) * (1.0 / GLA_GATE_NORM)
        if n_pad:
            row = t_in_seq * tile + lax.broadcasted_iota(jnp.int32, (tile, GLA_DK), 0)
            la = jnp.where(row >= n_pad, la, 0.0)
        la_scr[...] = la
        z_scr[...] = jnp.dot(u, wqkvg_ref[...], preferred_element_type=F32)
        _gla_decay(la_scr, dec_scr, etot_scr, tri_ref, n_chunks)

    def output_projection():
        return hbuf[...] + jnp.dot(og[...], wout_ref[...], preferred_element_type=F32)

    def mlp_and_gla(h1, with_gla=True):
        uu = _rms(h1, gf_ref[...]).astype(BF16)
        acc = h1
        gla_chunks = n_chunks if with_gla else 0
        for c in range(max(N_FF, gla_chunks)):
            if c < N_FF:
                a = jnp.maximum(jnp.dot(uu, w1c_ref[c], preferred_element_type=F32), 0.0)
            if c < gla_chunks:
                _gla_update(c, c * CHUNK, z_scr, dec_scr, etot_scr, state)
            if c < N_FF:
                acc = acc + jnp.dot((a * a).astype(BF16), w2c_ref[c], preferred_element_type=F32)
            if c < gla_chunks:
                _gla_output(c * CHUNK, z_scr, og, state, hg_ref)
        if final_norm:
            acc = _rms(acc, fg_ref[...])
        out_ref[...] = acc

    if state_only:
        input_projection()
        for c in range(n_chunks):
            _gla_update(c, c * CHUNK, z_scr, dec_scr, etot_scr, state)
        out_ref[...] = x_ref[...]
    else:
        @pl.when(s == 0)
        def _():
            input_projection()
            for c in range(n_chunks):
                _gla_update(c, c * CHUNK, z_scr, dec_scr, etot_scr, state)
                _gla_output(c * CHUNK, z_scr, og, state, hg_ref)
            hbuf[...] = x_ref[...]

        @pl.when(jnp.logical_and(s > 0, s < n_tiles))
        def _():
            h1 = output_projection()
            input_projection()
            mlp_and_gla(h1)
            hbuf[...] = x_ref[...]

        @pl.when(s == n_tiles)
        def _():
            mlp_and_gla(output_projection(), with_gla=False)

    @pl.when(s == n_tiles - 1)
    def _():
        sfin_ref[...] = state[...]


def _layer1(x2d, seq, gm, wqkvg, wr, gw2, gb, hg, wout, s0, tri, gf, w1c, w2c, fg, *, tile, n_pad,
            final_norm, state_only=False):
    n, d = x2d.shape
    assert seq % tile == 0 and n % seq == 0 and tile % CHUNK == 0
    n_tiles = n // tile
    pipelined = not state_only
    layer = w1c.layer
    if state_only:
        wout = jnp.zeros((8, LANES), BF16)
        w1c = w2c = _Layer(jnp.zeros((1, 8, LANES), BF16), 0)
    assert n_tiles > 1 or state_only
    kern = functools.partial(_layer1_kernel, tile=tile, n_pad=n_pad, n_tiles=n_tiles,
                             tiles_per_seq=seq // tile, final_norm=final_norm, state_only=state_only,
                             layer=layer)
    consts, const_specs = _split_consts((gm, wqkvg, wr, gw2, gb, hg, wout, s0, tri, gf, fg))
    late, late_specs, late_scratch = _late_operands(w1c, w2c)
    state_shape = (GLA_HEADS, GLA_HK, GLA_HV)
    in_map, out_map = _tile_maps(n_tiles, pipelined)
    return pl.pallas_call(
        kern,
        grid=(n_tiles + pipelined,),
        in_specs=[pl.BlockSpec((tile, d), in_map)] + const_specs + late_specs,
        out_specs=[pl.BlockSpec((tile, d), out_map), _const_spec(state_shape)],
        out_shape=[jax.ShapeDtypeStruct(x2d.shape, F32), jax.ShapeDtypeStruct(state_shape, F32)],
        scratch_shapes=[pltpu.VMEM((tile, N_QKVG), F32),
                        pltpu.VMEM((tile, GLA_DK), F32),
                        pltpu.VMEM((tile, GLA_DK), F32),
                        pltpu.VMEM((tile // CHUNK, 8, GLA_DK), F32),
                        pltpu.VMEM((tile, GLA_DV), BF16),
                        pltpu.VMEM(state_shape, F32),
                        pltpu.VMEM((tile, d), F32)] + late_scratch,
        compiler_params=pltpu.CompilerParams(dimension_semantics=("arbitrary",),
                                             vmem_limit_bytes=VMEM_LIMIT),
        name="layer1",
    )(x2d, *consts, *late)


def kernel(x, meta_tokens, mix_norm_g, ffn_norm_g, ffn_w1, ffn_w2, cp_w_in, cp_conv_w, cp_conv_b,
           cp_ln_g, cp_ln_b, cp_pool_w, cp_pool_scale, cp_w_out, gla_w_in, gla_gate_w2, gla_gate_b,
           gla_head_g, gla_w_out, final_norm_g):
    bsz, seq, d = x.shape
    row = lambda v: v.reshape(1, -1).astype(F32)

    depth = ffn_w1.shape[0]
    w1c = _to_bf16_col_chunks(ffn_w1)
    w2c = _to_bf16(ffn_w2, (2 * FF_CHUNK, d)).reshape(depth, N_FF, FF_CHUNK, d)
    cp_win = _Layer(_to_bf16(cp_w_in, (d, FF_CHUNK)), 0)
    cp_wout = _Layer(_to_bf16(cp_w_out, (d, FF_CHUNK)), 0)
    pw_bd = jnp.zeros((POOL_DIM, POOL_DIM), F32)
    for gi in range(len(POOL_WINDOWS)):
        lo = gi * POOL_GROUP
        pw_bd = lax.dynamic_update_slice(pw_bd, cp_pool_w[0, gi], (lo, lo))
    pw_bd = pw_bd.astype(BF16)
    wqkvg = gla_w_in[0, :, :N_QKVG].astype(BF16)
    wr = jnp.pad(gla_w_in[0, :, N_QKVG:], ((0, 0), (0, LANES - GLA_GATE_RANK))).astype(BF16)
    gw2 = jnp.pad(gla_gate_w2[0], ((0, LANES - GLA_GATE_RANK), (0, 0))).astype(BF16)
    gla_wout = _Layer(_to_bf16(gla_w_out, (d, FF_CHUNK)), 0)
    tri = jnp.tril(jnp.ones((CHUNK, CHUNK), BF16))
    fg = row(final_norm_g)

    l0_args = (row(mix_norm_g[0]), cp_win, cp_conv_w[0].astype(F32), row(cp_conv_b[0]),
               row(cp_ln_g[0]), row(cp_ln_b[0]), pw_bd, row(cp_pool_scale[0]), cp_wout)
    l0_mlp = (row(ffn_norm_g[0]), _Layer(w1c, 0), _Layer(w2c, 0))
    l1_args = (row(mix_norm_g[1]), wqkvg, wr, gw2, row(gla_gate_b[0]), row(gla_head_g[0]), gla_wout)
    l1_mlp = (row(ffn_norm_g[1]), _Layer(w1c, 1), _Layer(w2c, 1), fg)

    hm, a_tail, p_tail = _layer0(meta_tokens.astype(F32), N_META, *l0_args,
                                 jnp.zeros((A_CARRY, CONV_DIM), F32),
                                 jnp.zeros((P_CARRY, POOL_DIM), F32), *l0_mlp, tile=N_META, l_off=0)
    hm = jnp.pad(hm, ((CHUNK - N_META, 0), (0, 0)))
    s_zero = jnp.zeros((GLA_HEADS, GLA_HK, GLA_HV), F32)
    _, s_meta = _layer1(hm, CHUNK, *l1_args, s_zero, tri, *l1_mlp, tile=CHUNK,
                        n_pad=CHUNK - N_META, final_norm=False, state_only=True)

    tile = 512
    h, _, _ = _layer0(x.astype(F32).reshape(bsz * seq, d), seq, *l0_args, a_tail, p_tail, *l0_mlp,
                      tile=tile, l_off=N_META)
    out, _ = _layer1(h, seq, *l1_args, s_meta, tri, *l1_mlp, tile=tile, n_pad=0, final_norm=True)
    return out.reshape(bsz, seq, d).astype(x.dtype)
```

```python
import functools

import jax
import jax.numpy as jnp
from jax import lax
from jax.experimental import pallas as pl
from jax.experimental.pallas import tpu as pltpu

D_MODEL = 1024
N_META = 16
CHUNK = 64
D_FF = 4 * D_MODEL
EPS = 1e-5

CONV_DIM = D_MODEL // 2
CONV_WIDTH = 31
POOL_DIM = D_MODEL // 2
POOL_WINDOWS = (2, 4, 8, 16)
POOL_GROUP = POOL_DIM // len(POOL_WINDOWS)

GLA_HEADS = 4
GLA_DK = D_MODEL // 2
GLA_DV = D_MODEL
GLA_HK = GLA_DK // GLA_HEADS
GLA_HV = GLA_DV // GLA_HEADS
GLA_GATE_RANK = 16
GLA_GATE_NORM = 16.0
N_QKVG = 2 * GLA_DK + 2 * GLA_DV

LANES = 128
N_SLAB = CONV_DIM // LANES
assert POOL_GROUP == LANES and POOL_DIM == CONV_DIM
assert N_META >= max(POOL_WINDOWS)
A_CARRY = 32
P_CARRY = 16
ROW_BLOCK = 64
FF_CHUNK = 512
N_FF = D_FF // FF_CHUNK
VMEM_LIMIT = 56 * 1024 * 1024

F32 = jnp.float32
BF16 = jnp.bfloat16


def _rms(x, g):
    return x * lax.rsqrt(jnp.mean(x * x, axis=-1, keepdims=True) + EPS) * g


def _sigmoid(x):
    return 1.0 / (1.0 + jnp.exp(-x))


def _const_spec(shape):
    zeros = (0,) * len(shape)
    return pl.BlockSpec(shape, lambda *_: zeros)


class _Layer:
    def __init__(self, array, layer):
        self.array, self.layer = array, layer
        self.shape = array.shape[1:]

    def spec(self):
        index = (self.layer,) + (0,) * len(self.shape)
        return pl.BlockSpec((None,) + self.shape, lambda *_: index, pipeline_mode=pl.Buffered(1))


def _split_consts(consts):
    arrays = [c.array if isinstance(c, _Layer) else c for c in consts]
    specs = [c.spec() if isinstance(c, _Layer) else _const_spec(c.shape) for c in consts]
    return arrays, specs


def _late_operands(*layers):
    arrays = [w.array for w in layers]
    specs = [pl.BlockSpec(memory_space=pl.ANY) for _ in layers]
    scratch = [pltpu.VMEM(w.shape, w.array.dtype) for w in layers]
    return arrays, specs, scratch + [pltpu.SemaphoreType.DMA((len(layers),))]


def _cast_kernel(x_ref, o_ref):
    o_ref[...] = x_ref[...].astype(o_ref.dtype)


def _cast_split_kernel(x_ref, o_ref):
    for j in range(o_ref.shape[1]):
        o_ref[0, j] = x_ref[0, :, j * FF_CHUNK:(j + 1) * FF_CHUNK].astype(o_ref.dtype)


def _to_bf16_col_chunks(w):
    layers, rows, cols = w.shape
    per_step = 2
    n_chunks = cols // FF_CHUNK
    assert cols % (per_step * FF_CHUNK) == 0
    return pl.pallas_call(
        _cast_split_kernel,
        grid=(layers, n_chunks // per_step),
        in_specs=[pl.BlockSpec((1, rows, per_step * FF_CHUNK), lambda l, j: (l, 0, j))],
        out_specs=pl.BlockSpec((1, per_step, rows, FF_CHUNK), lambda l, j: (l, j, 0, 0)),
        out_shape=jax.ShapeDtypeStruct((layers, n_chunks, rows, FF_CHUNK), BF16),
        name="to_bf16_chunks",
    )(w)


def _to_bf16(w, block):
    layers, rows, cols = w.shape
    br, bc = block
    assert rows % br == 0 and cols % bc == 0
    spec = pl.BlockSpec((1, br, bc), lambda l, i, j: (l, i, j))
    return pl.pallas_call(
        _cast_kernel,
        grid=(layers, rows // br, cols // bc),
        in_specs=[spec],
        out_specs=spec,
        out_shape=jax.ShapeDtypeStruct((layers, rows, cols), BF16),
        name="to_bf16",
    )(w)


def _tile_maps(n_tiles):
    return (lambda s: (jnp.minimum(s, n_tiles - 1), 0)), (lambda s: (jnp.maximum(s - 1, 0), 0))


def _dependent_zero(v):
    bits = lax.bitcast_convert_type(v, jnp.uint32)
    bits = lax.shift_right_logical(lax.shift_right_logical(bits, jnp.uint32(16)), jnp.uint32(16))
    return lax.bitcast_convert_type(bits, F32)


def _delayed(token, hops=3):
    for _ in range(hops):
        s = jnp.sum(token, axis=-1, keepdims=True)
        token = _dependent_zero(jnp.broadcast_to(s, token.shape))
    return token


def _late_weight_copies(layer, w1c_hbm, w2c_hbm, w1c_ref, w2c_ref, sem):
    return (pltpu.make_async_copy(w1c_hbm.at[layer], w1c_ref, sem.at[0]),
            pltpu.make_async_copy(w2c_hbm.at[layer], w2c_ref, sem.at[1]))


def _slabs(x):
    return jnp.stack([x[:, k * LANES:(k + 1) * LANES] for k in range(N_SLAB)])


def _conv_pool_block(r0, rb, abuf, pbuf, cbuf, mix, cw_ref, cb_ref, lng_ref, lnb_ref, seq_start):
    for k in range(N_SLAB):
        cols = slice(k * LANES, (k + 1) * LANES)
        acc = jnp.broadcast_to(cb_ref[:, cols], (rb, LANES))
        for j in range(CONV_WIDTH):
            off = A_CARRY - (CONV_WIDTH - 1) + j
            acc = acc + cw_ref[j:j + 1, cols] * abuf[k, pl.ds(r0 + off, rb), :]
        cbuf[k, pl.ds(r0, rb), :] = acc
    conv = [cbuf[k, pl.ds(r0, rb), :] for k in range(N_SLAB)]
    mu = sum(jnp.sum(c, axis=-1, keepdims=True) for c in conv) * (1.0 / CONV_DIM)
    cen = [c - mu for c in conv]
    var = sum(jnp.sum(c * c, axis=-1, keepdims=True) for c in cen) * (1.0 / CONV_DIM)
    rstd = lax.rsqrt(var + EPS)
    for k in range(N_SLAB):
        cols = slice(k * LANES, (k + 1) * LANES)
        y = cen[k] * rstd * lng_ref[:, cols] + lnb_ref[:, cols]
        mix[pl.ds(r0, rb), cols] = (y * _sigmoid(y)).astype(BF16)
    for k, w in enumerate(POOL_WINDOWS):
        win = pbuf[k, pl.ds(r0, P_CARRY + rb), :]
        cur = win[P_CARRY:]
        ws = win
        for step in range(w.bit_length() - 1):
            ws = ws + pltpu.roll(ws, 1 << step, axis=0)
        ws = ws[P_CARRY:]
        if seq_start:
            pos = r0 + lax.broadcasted_iota(jnp.int32, (rb, LANES), 0)
            mean = ws / jnp.minimum(pos + 1, w).astype(F32)
        else:
            mean = ws * (1.0 / w)
        mix[pl.ds(r0, rb), CONV_DIM + k * LANES:CONV_DIM + (k + 1) * LANES] = (mean - cur).astype(BF16)


def _layer0_kernel(x_ref, meta_ref, gm_ref, win_ref, cw_ref, cb_ref, lng_ref, lnb_ref, pw_ref, ps_ref,
                   wout_ref, gf_ref, w1c_hbm, w2c_hbm, out_ref, hm_ref,
                   abuf, pbuf, cbuf, mix, hbuf, a0s, p0s, w1c_ref, w2c_ref, sem, *, tile, n_tiles,
                   tiles_per_seq, layer):
    s = pl.program_id(0)
    t_in_seq = lax.rem(s, tiles_per_seq)
    rb = min(ROW_BLOCK, tile)
    mlp_weights = _late_weight_copies(layer, w1c_hbm, w2c_hbm, w1c_ref, w2c_ref, sem)

    def conv_blocks(n_rows, block, seq_start=False):
        for i in range(n_rows // block):
            _conv_pool_block(i * block, block, abuf, pbuf, cbuf, mix, cw_ref, cb_ref, lng_ref,
                             lnb_ref, seq_start)

    def input_projection(xv):
        u = _rms(xv, gm_ref[...]).astype(BF16)
        z = jnp.dot(u, win_ref[...], preferred_element_type=F32)
        return z[:, :CONV_DIM] * _sigmoid(z[:, CONV_DIM:2 * CONV_DIM]), z[:, 2 * CONV_DIM:]

    def output_projection(n_rows):
        pooled = jnp.dot(mix[0:n_rows, CONV_DIM:], pw_ref[...], preferred_element_type=F32)
        pooled = (pooled * ps_ref[...]).astype(BF16)
        y = jnp.dot(mix[0:n_rows, 0:CONV_DIM], wout_ref[0:CONV_DIM, :], preferred_element_type=F32)
        return y + jnp.dot(pooled, wout_ref[CONV_DIM:, :], preferred_element_type=F32)

    def mlp(h1):
        uu = _rms(h1, gf_ref[...]).astype(BF16)
        acc = h1
        for c in range(N_FF):
            a = jnp.maximum(jnp.dot(uu, w1c_ref[c], preferred_element_type=F32), 0.0)
            acc = acc + jnp.dot((a * a).astype(BF16), w2c_ref[c], preferred_element_type=F32)
        return acc

    def load_carries():
        abuf[:, 0:A_CARRY, :] = a0s[...]
        pbuf[:, 0:P_CARRY, :] = p0s[...]

    def shift_carries():
        a_tail = abuf[:, tile:tile + A_CARRY, :]
        p_tail = pbuf[:, tile:tile + P_CARRY, :]
        abuf[:, 0:A_CARRY, :] = a_tail
        pbuf[:, 0:P_CARRY, :] = p_tail

    def meta_mixer():
        abuf[:, 0:A_CARRY, :] = jnp.zeros((N_SLAB, A_CARRY, LANES), F32)
        pbuf[:, 0:P_CARRY, :] = jnp.zeros((N_SLAB, P_CARRY, LANES), F32)
        glu, pin = input_projection(meta_ref[...])
        abuf[:, A_CARRY:A_CARRY + N_META, :] = _slabs(glu)
        pbuf[:, P_CARRY:P_CARRY + N_META, :] = _slabs(pin)
        conv_blocks(N_META, N_META, seq_start=True)
        a0s[...] = abuf[:, N_META:N_META + A_CARRY, :]
        p0s[...] = pbuf[:, N_META:N_META + P_CARRY, :]
        return meta_ref[...] + output_projection(N_META)

    def front_only():
        glu, pin = input_projection(x_ref[...])
        abuf[:, A_CARRY:A_CARRY + tile, :] = _slabs(glu)
        pbuf[:, P_CARRY:P_CARRY + tile, :] = _slabs(pin)
        conv_blocks(tile, rb)

    def front_and_back():
        y = output_projection(tile)
        glu, pin = input_projection(x_ref[...])
        h1 = hbuf[...] + y
        token = _dependent_zero(glu[0:8, 0:LANES])
        for i in range(tile // rb):
            rows = slice(i * rb, (i + 1) * rb)
            for k in range(N_SLAB):
                zero = jnp.tile(token, (rb // 8, 1))
                cols = slice(k * LANES, (k + 1) * LANES)
                abuf[k, A_CARRY + i * rb:A_CARRY + (i + 1) * rb, :] = glu[rows, cols] + zero
                pbuf[k, P_CARRY + i * rb:P_CARRY + (i + 1) * rb, :] = pin[rows, cols] + zero
                token = _delayed(token)
        out_ref[...] = mlp(h1)
        conv_blocks(tile, rb)

    @pl.when(s == 0)
    def _():
        for cp in mlp_weights:
            cp.start()
        h1_meta = meta_mixer()
        load_carries()
        front_only()
        hbuf[...] = x_ref[...]
        shift_carries()
        for cp in mlp_weights:
            cp.wait()
        hm_ref[...] = mlp(h1_meta)

    @pl.when(jnp.logical_and(s > 0, t_in_seq == 0))
    def _():
        load_carries()

    @pl.when(jnp.logical_and(s > 0, s < n_tiles))
    def _():
        front_and_back()
        hbuf[...] = x_ref[...]
        shift_carries()

    @pl.when(s == n_tiles)
    def _():
        out_ref[...] = mlp(hbuf[...] + output_projection(tile))


def _layer0(x2d, seq, meta, gm, win, cw, cb, lng, lnb, pw_bd, ps, wout, gf, w1c, w2c, *, tile):
    n, d = x2d.shape
    assert seq % tile == 0 and n % seq == 0 and tile % ROW_BLOCK == 0 and n > tile
    n_tiles = n // tile
    kern = functools.partial(_layer0_kernel, tile=tile, n_tiles=n_tiles, tiles_per_seq=seq // tile,
                             layer=w1c.layer)
    consts, const_specs = _split_consts((meta, gm, win, cw, cb, lng, lnb, pw_bd, ps, wout, gf))
    late, late_specs, late_scratch = _late_operands(w1c, w2c)
    in_map, out_map = _tile_maps(n_tiles)
    return pl.pallas_call(
        kern,
        grid=(n_tiles + 1,),
        in_specs=[pl.BlockSpec((tile, d), in_map)] + const_specs + late_specs,
        out_specs=[pl.BlockSpec((tile, d), out_map), _const_spec(meta.shape)],
        out_shape=[jax.ShapeDtypeStruct(x2d.shape, F32), jax.ShapeDtypeStruct(meta.shape, F32)],
        scratch_shapes=[pltpu.VMEM((N_SLAB, A_CARRY + tile, LANES), F32),
                        pltpu.VMEM((N_SLAB, P_CARRY + tile, LANES), F32),
                        pltpu.VMEM((N_SLAB, tile, LANES), F32),
                        pltpu.VMEM((tile, CONV_DIM + POOL_DIM), BF16),
                        pltpu.VMEM((tile, d), F32),
                        pltpu.VMEM((N_SLAB, A_CARRY, LANES), F32),
                        pltpu.VMEM((N_SLAB, P_CARRY, LANES), F32)] + late_scratch,
        compiler_params=pltpu.CompilerParams(dimension_semantics=("arbitrary",),
                                             vmem_limit_bytes=VMEM_LIMIT),
        name="layer0",
    )(x2d, *consts, *late)


def _chunk_cumsum(tri_ref, x):
    tri = tri_ref[...]
    total = None
    for _ in range(3):
        part = x.astype(BF16)
        x = x - part.astype(F32)
        term = jnp.dot(tri, part, preferred_element_type=F32)
        total = term if total is None else total + term
    return total


def _gla_decay(la_scr, dec_scr, etot_scr, tri_ref, n_chunks):
    for c in range(n_chunks):
        rows = slice(c * CHUNK, (c + 1) * CHUNK)
        cum = _chunk_cumsum(tri_ref, la_scr[rows, :])
        tot = cum[CHUNK - 1:CHUNK, :]
        dec_scr[rows, :] = jnp.exp(tot - cum)
        etot_scr[c] = jnp.broadcast_to(jnp.exp(tot), (8, GLA_DK))


def _gla_update(c, r0, z_scr, dec_scr, etot_scr, state):
    rows = pl.ds(r0, CHUNK)
    kd = (z_scr[rows, GLA_DK:2 * GLA_DK] * dec_scr[rows, :]).astype(BF16)
    etot = etot_scr[c][0:1, :]
    for hh in range(GLA_HEADS):
        klo, vlo = hh * GLA_HK, 2 * GLA_DK + hh * GLA_HV
        v_h = z_scr[rows, vlo:vlo + GLA_HV].astype(BF16)
        upd = lax.dot_general(kd[:, klo:klo + GLA_HK], v_h, (((0,), (0,)), ((), ())),
                              preferred_element_type=F32)
        ecol = jnp.transpose(jnp.broadcast_to(etot[:, klo:klo + GLA_HK], (GLA_HK, GLA_HK)))
        state[hh] = state[hh] * jnp.tile(ecol, (1, GLA_HV // GLA_HK)) + upd


def _gla_output(r0, z_scr, og, state, hg_ref):
    rows = pl.ds(r0, CHUNK)
    q = (z_scr[rows, 0:GLA_DK] * (GLA_HK ** -0.5)).astype(BF16)
    glo = 2 * GLA_DK + GLA_DV
    for hh in range(GLA_HEADS):
        klo = hh * GLA_HK
        o = jnp.dot(q[:, klo:klo + GLA_HK], state[hh].astype(BF16),
                    preferred_element_type=F32)
        o = o * lax.rsqrt(jnp.mean(o * o, axis=-1, keepdims=True) + EPS) * hg_ref[...]
        gt = z_scr[rows, glo + hh * GLA_HV:glo + (hh + 1) * GLA_HV]
        og[rows, hh * GLA_HV:(hh + 1) * GLA_HV] = (o * (gt * _sigmoid(gt))).astype(BF16)


def _layer1_kernel(x_ref, hm_ref, gm_ref, wqkvg_ref, wr_ref, gw2_ref, gb_ref, hg_ref, wout_ref,
                   tri_ref, gf_ref, fg_ref, w1c_hbm, w2c_hbm, out_ref,
                   z_scr, la_scr, dec_scr, etot_scr, og, state, s0s, hbuf, w1c_ref, w2c_ref, sem, *,
                   tile, n_tiles, tiles_per_seq, layer):
    s = pl.program_id(0)
    t_in_seq = lax.rem(s, tiles_per_seq)
    n_chunks = tile // CHUNK
    mlp_weights = _late_weight_copies(layer, w1c_hbm, w2c_hbm, w1c_ref, w2c_ref, sem)

    def input_projection(xv, n_pad=0):
        n_rows = xv.shape[0]
        u = _rms(xv, gm_ref[...]).astype(BF16)
        r = jnp.dot(u, wr_ref[...], preferred_element_type=F32)
        x = jnp.dot(r.astype(BF16), gw2_ref[...], preferred_element_type=F32) + gb_ref[...]
        la = (jnp.minimum(x, 0.0) - jnp.log1p(jnp.exp(-jnp.abs(x)))) * (1.0 / GLA_GATE_NORM)
        if n_pad:
            row = lax.broadcasted_iota(jnp.int32, (n_rows, GLA_DK), 0)
            la = jnp.where(row >= n_pad, la, 0.0)
        la_scr[0:n_rows, :] = la
        z_scr[0:n_rows, :] = jnp.dot(u, wqkvg_ref[...], preferred_element_type=F32)
        _gla_decay(la_scr, dec_scr, etot_scr, tri_ref, n_rows // CHUNK)

    def meta_state():
        state[...] = jnp.zeros(state.shape, F32)
        pad = jnp.zeros((CHUNK - N_META, D_MODEL), F32)
        input_projection(jnp.concatenate([pad, hm_ref[...]], axis=0), n_pad=CHUNK - N_META)
        _gla_update(0, 0, z_scr, dec_scr, etot_scr, state)
        s0s[...] = state[...]

    def output_projection():
        return hbuf[...] + jnp.dot(og[...], wout_ref[...], preferred_element_type=F32)

    def mlp_and_gla(h1, with_gla=True):
        uu = _rms(h1, gf_ref[...]).astype(BF16)
        acc = h1
        gla_chunks = n_chunks if with_gla else 0
        for c in range(max(N_FF, gla_chunks)):
            if c < N_FF:
                a = jnp.maximum(jnp.dot(uu, w1c_ref[c], preferred_element_type=F32), 0.0)
            if c < gla_chunks:
                _gla_update(c, c * CHUNK, z_scr, dec_scr, etot_scr, state)
            if c < N_FF:
                acc = acc + jnp.dot((a * a).astype(BF16), w2c_ref[c], preferred_element_type=F32)
            if c < gla_chunks:
                _gla_output(c * CHUNK, z_scr, og, state, hg_ref)
        out_ref[...] = _rms(acc, fg_ref[...])

    @pl.when(s == 0)
    def _():
        for cp in mlp_weights:
            cp.start()
        meta_state()
        input_projection(x_ref[...])
        for c in range(n_chunks):
            _gla_update(c, c * CHUNK, z_scr, dec_scr, etot_scr, state)
            _gla_output(c * CHUNK, z_scr, og, state, hg_ref)
        hbuf[...] = x_ref[...]

    @pl.when(s == 1)
    def _():
        for cp in mlp_weights:
            cp.wait()

    @pl.when(jnp.logical_and(s > 0, t_in_seq == 0))
    def _():
        state[...] = s0s[...]

    @pl.when(jnp.logical_and(s > 0, s < n_tiles))
    def _():
        h1 = output_projection()
        input_projection(x_ref[...])
        mlp_and_gla(h1)
        hbuf[...] = x_ref[...]

    @pl.when(s == n_tiles)
    def _():
        mlp_and_gla(output_projection(), with_gla=False)


def _layer1(x2d, seq, hm, gm, wqkvg, wr, gw2, gb, hg, wout, tri, gf, w1c, w2c, fg, *, tile):
    n, d = x2d.shape
    assert seq % tile == 0 and n % seq == 0 and tile % CHUNK == 0 and n > tile
    n_tiles = n // tile
    kern = functools.partial(_layer1_kernel, tile=tile, n_tiles=n_tiles, tiles_per_seq=seq // tile,
                             layer=w1c.layer)
    consts, const_specs = _split_consts((hm, gm, wqkvg, wr, gw2, gb, hg, wout, tri, gf, fg))
    late, late_specs, late_scratch = _late_operands(w1c, w2c)
    state_shape = (GLA_HEADS, GLA_HK, GLA_HV)
    in_map, out_map = _tile_maps(n_tiles)
    return pl.pallas_call(
        kern,
        grid=(n_tiles + 1,),
        in_specs=[pl.BlockSpec((tile, d), in_map)] + const_specs + late_specs,
        out_specs=pl.BlockSpec((tile, d), out_map),
        out_shape=jax.ShapeDtypeStruct(x2d.shape, F32),
        scratch_shapes=[pltpu.VMEM((tile, N_QKVG), F32),
                        pltpu.VMEM((tile, GLA_DK), F32),
                        pltpu.VMEM((tile, GLA_DK), F32),
                        pltpu.VMEM((tile // CHUNK, 8, GLA_DK), F32),
                        pltpu.VMEM((tile, GLA_DV), BF16),
                        pltpu.VMEM(state_shape, F32),
                        pltpu.VMEM(state_shape, F32),
                        pltpu.VMEM((tile, d), F32)] + late_scratch,
        compiler_params=pltpu.CompilerParams(dimension_semantics=("arbitrary",),
                                             vmem_limit_bytes=VMEM_LIMIT),
        name="layer1",
    )(x2d, *consts, *late)


def kernel(x, meta_tokens, mix_norm_g, ffn_norm_g, ffn_w1, ffn_w2, cp_w_in, cp_conv_w, cp_conv_b,
           cp_ln_g, cp_ln_b, cp_pool_w, cp_pool_scale, cp_w_out, gla_w_in, gla_gate_w2, gla_gate_b,
           gla_head_g, gla_w_out, final_norm_g):
    bsz, seq, d = x.shape
    row = lambda v: v.reshape(1, -1).astype(F32)

    depth = ffn_w1.shape[0]
    w1c = _to_bf16_col_chunks(ffn_w1)
    w2c = _to_bf16(ffn_w2, (2 * FF_CHUNK, d)).reshape(depth, N_FF, FF_CHUNK, d)
    cp_win = _Layer(_to_bf16(cp_w_in, (d, FF_CHUNK)), 0)
    cp_wout = _Layer(_to_bf16(cp_w_out, (d, FF_CHUNK)), 0)
    pw_bd = jnp.zeros((POOL_DIM, POOL_DIM), F32)
    for gi in range(len(POOL_WINDOWS)):
        lo = gi * POOL_GROUP
        pw_bd = lax.dynamic_update_slice(pw_bd, cp_pool_w[0, gi], (lo, lo))
    pw_bd = pw_bd.astype(BF16)
    wqkvg = gla_w_in[0, :, :N_QKVG].astype(BF16)
    wr = jnp.pad(gla_w_in[0, :, N_QKVG:], ((0, 0), (0, LANES - GLA_GATE_RANK))).astype(BF16)
    gw2 = jnp.pad(gla_gate_w2[0], ((0, LANES - GLA_GATE_RANK), (0, 0))).astype(BF16)
    gla_wout = _Layer(_to_bf16(gla_w_out, (d, FF_CHUNK)), 0)
    tri = jnp.tril(jnp.ones((CHUNK, CHUNK), BF16))
    fg = row(final_norm_g)

    l0_args = (row(mix_norm_g[0]), cp_win, cp_conv_w[0].astype(F32), row(cp_conv_b[0]),
               row(cp_ln_g[0]), row(cp_ln_b[0]), pw_bd, row(cp_pool_scale[0]), cp_wout)
    l0_mlp = (row(ffn_norm_g[0]), _Layer(w1c, 0), _Layer(w2c, 0))
    l1_args = (row(mix_norm_g[1]), wqkvg, wr, gw2, row(gla_gate_b[0]), row(gla_head_g[0]), gla_wout)
    l1_mlp = (row(ffn_norm_g[1]), _Layer(w1c, 1), _Layer(w2c, 1), fg)

    tile = 512
    h, hm = _layer0(x.astype(F32).reshape(bsz * seq, d), seq, meta_tokens.astype(F32), *l0_args,
                    *l0_mlp, tile=tile)
    out = _layer1(h, seq, hm, *l1_args, tri, *l1_mlp, tile=tile)
    return out.reshape(bsz, seq, d).astype(x.dtype)
```

```python
import functools

import jax
import jax.numpy as jnp
from jax import lax
from jax.experimental import pallas as pl
from jax.experimental.pallas import tpu as pltpu

D_MODEL = 1024
N_META = 16
CHUNK = 64
D_FF = 4 * D_MODEL
EPS = 1e-5

CONV_DIM = D_MODEL // 2
CONV_WIDTH = 31
POOL_DIM = D_MODEL // 2
POOL_WINDOWS = (2, 4, 8, 16)
POOL_GROUP = POOL_DIM // len(POOL_WINDOWS)

GLA_HEADS = 4
GLA_DK = D_MODEL // 2
GLA_DV = D_MODEL
GLA_HK = GLA_DK // GLA_HEADS
GLA_HV = GLA_DV // GLA_HEADS
GLA_GATE_RANK = 16
GLA_GATE_NORM = 16.0
N_QKVG = 2 * GLA_DK + 2 * GLA_DV

LANES = 128
N_SLAB = CONV_DIM // LANES
assert POOL_GROUP == LANES and POOL_DIM == CONV_DIM
assert N_META >= max(POOL_WINDOWS)
A_CARRY = 32
P_CARRY = 16
ROW_BLOCK = 64
FF_CHUNK = 512
N_FF = D_FF // FF_CHUNK
VMEM_LIMIT = 56 * 1024 * 1024

F32 = jnp.float32
BF16 = jnp.bfloat16
_NT = (((1,), (1,)), ((), ()))


def _rms(x, g):
    return x * lax.rsqrt(jnp.mean(x * x, axis=-1, keepdims=True) + EPS) * g


def _sigmoid(x):
    return 1.0 / (1.0 + jnp.exp(-x))


def _const_spec(shape):
    zeros = (0,) * len(shape)
    return pl.BlockSpec(shape, lambda *_: zeros)


class _Layer:
    def __init__(self, array, layer):
        self.array, self.layer = array, layer
        self.shape = array.shape[1:]

    def spec(self):
        index = (self.layer,) + (0,) * len(self.shape)
        return pl.BlockSpec((None,) + self.shape, lambda *_: index, pipeline_mode=pl.Buffered(1))


def _split_consts(consts):
    arrays = [c.array if isinstance(c, _Layer) else c for c in consts]
    specs = [c.spec() if isinstance(c, _Layer) else _const_spec(c.shape) for c in consts]
    return arrays, specs


def _late_operands(*layers):
    arrays = [w.array for w in layers]
    specs = [pl.BlockSpec(memory_space=pl.ANY) for _ in layers]
    scratch = [pltpu.VMEM(w.shape, w.array.dtype) for w in layers]
    return arrays, specs, scratch + [pltpu.SemaphoreType.DMA((len(layers),))]


def _cast_kernel(x_ref, o_ref):
    o_ref[...] = x_ref[...].astype(o_ref.dtype)


def _cast_split_kernel(x_ref, o_ref):
    for j in range(o_ref.shape[1]):
        o_ref[0, j] = x_ref[0, :, j * FF_CHUNK:(j + 1) * FF_CHUNK].astype(o_ref.dtype)


def _to_bf16_col_chunks(w):
    layers, rows, cols = w.shape
    per_step = 2
    n_chunks = cols // FF_CHUNK
    assert cols % (per_step * FF_CHUNK) == 0
    return pl.pallas_call(
        _cast_split_kernel,
        grid=(layers, n_chunks // per_step),
        in_specs=[pl.BlockSpec((1, rows, per_step * FF_CHUNK), lambda l, j: (l, 0, j))],
        out_specs=pl.BlockSpec((1, per_step, rows, FF_CHUNK), lambda l, j: (l, j, 0, 0)),
        out_shape=jax.ShapeDtypeStruct((layers, n_chunks, rows, FF_CHUNK), BF16),
        name="to_bf16_chunks",
    )(w)


def _to_bf16(w, block):
    layers, rows, cols = w.shape
    br, bc = block
    assert rows % br == 0 and cols % bc == 0
    spec = pl.BlockSpec((1, br, bc), lambda l, i, j: (l, i, j))
    return pl.pallas_call(
        _cast_kernel,
        grid=(layers, rows // br, cols // bc),
        in_specs=[spec],
        out_specs=spec,
        out_shape=jax.ShapeDtypeStruct((layers, rows, cols), BF16),
        name="to_bf16",
    )(w)


def _tile_maps(n_tiles):
    return (lambda s: (jnp.minimum(s, n_tiles - 1), 0)), (lambda s: (jnp.maximum(s - 1, 0), 0))


def _dependent_zero(v):
    bits = lax.bitcast_convert_type(v, jnp.uint32)
    bits = lax.shift_right_logical(lax.shift_right_logical(bits, jnp.uint32(16)), jnp.uint32(16))
    return lax.bitcast_convert_type(bits, F32)


def _delayed(token, hops=3):
    for _ in range(hops):
        s = jnp.sum(token, axis=-1, keepdims=True)
        token = _dependent_zero(jnp.broadcast_to(s, token.shape))
    return token


def _late_weight_copies(layer, w1c_hbm, w2c_hbm, w1c_ref, w2c_ref, sem):
    return (pltpu.make_async_copy(w1c_hbm.at[layer], w1c_ref, sem.at[0]),
            pltpu.make_async_copy(w2c_hbm.at[layer], w2c_ref, sem.at[1]))


def _slabs(x):
    return jnp.stack([x[:, k * LANES:(k + 1) * LANES] for k in range(N_SLAB)])


def _conv_pool_block(r0, rb, abuf, pbuf, cbuf, mix, cw_ref, cb_ref, lng_ref, lnb_ref, seq_start):
    for k in range(N_SLAB):
        cols = slice(k * LANES, (k + 1) * LANES)
        acc = jnp.broadcast_to(cb_ref[:, cols], (rb, LANES))
        for j in range(CONV_WIDTH):
            off = A_CARRY - (CONV_WIDTH - 1) + j
            acc = acc + cw_ref[j:j + 1, cols] * abuf[k, pl.ds(r0 + off, rb), :]
        cbuf[k, pl.ds(r0, rb), :] = acc
    conv = [cbuf[k, pl.ds(r0, rb), :] for k in range(N_SLAB)]
    mu = sum(jnp.sum(c, axis=-1, keepdims=True) for c in conv) * (1.0 / CONV_DIM)
    cen = [c - mu for c in conv]
    var = sum(jnp.sum(c * c, axis=-1, keepdims=True) for c in cen) * (1.0 / CONV_DIM)
    rstd = lax.rsqrt(var + EPS)
    for k in range(N_SLAB):
        cols = slice(k * LANES, (k + 1) * LANES)
        y = cen[k] * rstd * lng_ref[:, cols] + lnb_ref[:, cols]
        mix[pl.ds(r0, rb), cols] = (y * _sigmoid(y)).astype(BF16)
    for k, w in enumerate(POOL_WINDOWS):
        win = pbuf[k, pl.ds(r0, P_CARRY + rb), :]
        cur = win[P_CARRY:]
        ws = win
        for step in range(w.bit_length() - 1):
            ws = ws + pltpu.roll(ws, 1 << step, axis=0)
        ws = ws[P_CARRY:]
        if seq_start:
            pos = r0 + lax.broadcasted_iota(jnp.int32, (rb, LANES), 0)
            mean = ws / jnp.minimum(pos + 1, w).astype(F32)
        else:
            mean = ws * (1.0 / w)
        mix[pl.ds(r0, rb), CONV_DIM + k * LANES:CONV_DIM + (k + 1) * LANES] = (mean - cur).astype(BF16)


def _layer0_kernel(x_ref, meta_ref, gm_ref, win_ref, cw_ref, cb_ref, lng_ref, lnb_ref, pw_ref, ps_ref,
                   wout_ref, gf_ref, w1c_hbm, w2c_hbm, out_ref, hm_ref,
                   abuf, pbuf, cbuf, mix, hbuf, a0s, p0s, w1c_ref, w2c_ref, sem, *, tile, n_tiles,
                   tiles_per_seq, layer):
    s = pl.program_id(0)
    t_in_seq = lax.rem(s, tiles_per_seq)
    rb = min(ROW_BLOCK, tile)
    mlp_weights = _late_weight_copies(layer, w1c_hbm, w2c_hbm, w1c_ref, w2c_ref, sem)

    def conv_blocks(n_rows, block, seq_start=False):
        for i in range(n_rows // block):
            _conv_pool_block(i * block, block, abuf, pbuf, cbuf, mix, cw_ref, cb_ref, lng_ref,
                             lnb_ref, seq_start)

    def input_projection(xv):
        u = _rms(xv, gm_ref[...]).astype(BF16)
        z = jnp.dot(u, win_ref[...], preferred_element_type=F32)
        return z[:, :CONV_DIM] * _sigmoid(z[:, CONV_DIM:2 * CONV_DIM]), z[:, 2 * CONV_DIM:]

    def output_projection(n_rows):
        pooled = jnp.dot(mix[0:n_rows, CONV_DIM:], pw_ref[...], preferred_element_type=F32)
        pooled = (pooled * ps_ref[...]).astype(BF16)
        y = jnp.dot(mix[0:n_rows, 0:CONV_DIM], wout_ref[0:CONV_DIM, :], preferred_element_type=F32)
        return y + jnp.dot(pooled, wout_ref[CONV_DIM:, :], preferred_element_type=F32)

    def mlp(h1):
        uu = _rms(h1, gf_ref[...]).astype(BF16)
        acc = h1
        for c in range(N_FF):
            a = jnp.maximum(jnp.dot(uu, w1c_ref[c], preferred_element_type=F32), 0.0)
            acc = acc + jnp.dot((a * a).astype(BF16), w2c_ref[c], preferred_element_type=F32)
        return acc

    def load_carries():
        abuf[:, 0:A_CARRY, :] = a0s[...]
        pbuf[:, 0:P_CARRY, :] = p0s[...]

    def shift_carries():
        a_tail = abuf[:, tile:tile + A_CARRY, :]
        p_tail = pbuf[:, tile:tile + P_CARRY, :]
        abuf[:, 0:A_CARRY, :] = a_tail
        pbuf[:, 0:P_CARRY, :] = p_tail

    def meta_mixer():
        abuf[:, 0:A_CARRY, :] = jnp.zeros((N_SLAB, A_CARRY, LANES), F32)
        pbuf[:, 0:P_CARRY, :] = jnp.zeros((N_SLAB, P_CARRY, LANES), F32)
        glu, pin = input_projection(meta_ref[...])
        abuf[:, A_CARRY:A_CARRY + N_META, :] = _slabs(glu)
        pbuf[:, P_CARRY:P_CARRY + N_META, :] = _slabs(pin)
        conv_blocks(N_META, N_META, seq_start=True)
        a0s[...] = abuf[:, N_META:N_META + A_CARRY, :]
        p0s[...] = pbuf[:, N_META:N_META + P_CARRY, :]
        return meta_ref[...] + output_projection(N_META)

    def front_only():
        glu, pin = input_projection(x_ref[...])
        abuf[:, A_CARRY:A_CARRY + tile, :] = _slabs(glu)
        pbuf[:, P_CARRY:P_CARRY + tile, :] = _slabs(pin)
        conv_blocks(tile, rb)

    def front_and_back():
        y = output_projection(tile)
        glu, pin = input_projection(x_ref[...])
        h1 = hbuf[...] + y
        token = _dependent_zero(glu[0:8, 0:LANES])
        for i in range(tile // rb):
            rows = slice(i * rb, (i + 1) * rb)
            for k in range(N_SLAB):
                zero = jnp.tile(token, (rb // 8, 1))
                cols = slice(k * LANES, (k + 1) * LANES)
                abuf[k, A_CARRY + i * rb:A_CARRY + (i + 1) * rb, :] = glu[rows, cols] + zero
                pbuf[k, P_CARRY + i * rb:P_CARRY + (i + 1) * rb, :] = pin[rows, cols] + zero
                token = _delayed(token)
        out_ref[...] = mlp(h1)
        conv_blocks(tile, rb)

    @pl.when(s == 0)
    def _():
        for cp in mlp_weights:
            cp.start()
        h1_meta = meta_mixer()
        load_carries()
        front_only()
        hbuf[...] = x_ref[...]
        shift_carries()
        for cp in mlp_weights:
            cp.wait()
        hm_ref[...] = mlp(h1_meta)

    @pl.when(jnp.logical_and(s > 0, t_in_seq == 0))
    def _():
        load_carries()

    @pl.when(jnp.logical_and(s > 0, s < n_tiles))
    def _():
        front_and_back()
        hbuf[...] = x_ref[...]
        shift_carries()

    @pl.when(s == n_tiles)
    def _():
        out_ref[...] = mlp(hbuf[...] + output_projection(tile))


def _layer0(x2d, seq, meta, gm, win, cw, cb, lng, lnb, pw_bd, ps, wout, gf, w1c, w2c, *, tile):
    n, d = x2d.shape
    assert seq % tile == 0 and n % seq == 0 and tile % ROW_BLOCK == 0 and n > tile
    n_tiles = n // tile
    kern = functools.partial(_layer0_kernel, tile=tile, n_tiles=n_tiles, tiles_per_seq=seq // tile,
                             layer=w1c.layer)
    consts, const_specs = _split_consts((meta, gm, win, cw, cb, lng, lnb, pw_bd, ps, wout, gf))
    late, late_specs, late_scratch = _late_operands(w1c, w2c)
    in_map, out_map = _tile_maps(n_tiles)
    return pl.pallas_call(
        kern,
        grid=(n_tiles + 1,),
        in_specs=[pl.BlockSpec((tile, d), in_map)] + const_specs + late_specs,
        out_specs=[pl.BlockSpec((tile, d), out_map), _const_spec(meta.shape)],
        out_shape=[jax.ShapeDtypeStruct(x2d.shape, F32), jax.ShapeDtypeStruct(meta.shape, F32)],
        scratch_shapes=[pltpu.VMEM((N_SLAB, A_CARRY + tile, LANES), F32),
                        pltpu.VMEM((N_SLAB, P_CARRY + tile, LANES), F32),
                        pltpu.VMEM((N_SLAB, tile, LANES), F32),
                        pltpu.VMEM((tile, CONV_DIM + POOL_DIM), BF16),
                        pltpu.VMEM((tile, d), F32),
                        pltpu.VMEM((N_SLAB, A_CARRY, LANES), F32),
                        pltpu.VMEM((N_SLAB, P_CARRY, LANES), F32)] + late_scratch,
        compiler_params=pltpu.CompilerParams(dimension_semantics=("arbitrary",),
                                             vmem_limit_bytes=VMEM_LIMIT),
        name="layer0",
    )(x2d, *consts, *late)


def _chunk_cumsum(tri_ref, x):
    tri = tri_ref[...]
    total = None
    for _ in range(3):
        part = x.astype(BF16)
        x = x - part.astype(F32)
        term = jnp.dot(tri, part, preferred_element_type=F32)
        total = term if total is None else total + term
    return total


def _gla_decay(la_scr, dec_scr, etot_scr, tri_ref, n_chunks):
    for c in range(n_chunks):
        rows = slice(c * CHUNK, (c + 1) * CHUNK)
        cum = _chunk_cumsum(tri_ref, la_scr[rows, :])
        tot = cum[CHUNK - 1:CHUNK, :]
        dec_scr[rows, :] = jnp.exp(tot - cum)
        etot_scr[c] = jnp.broadcast_to(jnp.exp(tot), (8, GLA_DK))


def _gla_update(c, r0, z_scr, dec_scr, etot_scr, state):
    rows = pl.ds(r0, CHUNK)
    kd = (z_scr[rows, GLA_DK:2 * GLA_DK] * dec_scr[rows, :]).astype(BF16)
    etot = etot_scr[c][0:1, :]
    for hh in range(GLA_HEADS):
        klo, vlo = hh * GLA_HK, 2 * GLA_DK + hh * GLA_HV
        v_h = z_scr[rows, vlo:vlo + GLA_HV].astype(BF16)
        upd = lax.dot_general(kd[:, klo:klo + GLA_HK], v_h, (((0,), (0,)), ((), ())),
                              preferred_element_type=F32)
        ecol = jnp.transpose(jnp.broadcast_to(etot[:, klo:klo + GLA_HK], (GLA_HK, GLA_HK)))
        state[hh] = state[hh] * jnp.tile(ecol, (1, GLA_HV // GLA_HK)) + upd


def _gla_output(r0, z_scr, og, state, hg_ref):
    rows = pl.ds(r0, CHUNK)
    q = (z_scr[rows, 0:GLA_DK] * (GLA_HK ** -0.5)).astype(BF16)
    glo = 2 * GLA_DK + GLA_DV
    for hh in range(GLA_HEADS):
        klo = hh * GLA_HK
        o = jnp.dot(q[:, klo:klo + GLA_HK], state[hh].astype(BF16),
                    preferred_element_type=F32)
        o = o * lax.rsqrt(jnp.mean(o * o, axis=-1, keepdims=True) + EPS) * hg_ref[...]
        gt = z_scr[rows, glo + hh * GLA_HV:glo + (hh + 1) * GLA_HV]
        og[rows, hh * GLA_HV:(hh + 1) * GLA_HV] = (o * (gt * _sigmoid(gt))).astype(BF16)


def _layer1_kernel(x_ref, hm_ref, gm_ref, wqkvgt_ref, wrt_ref, gw2_ref, gb_ref, hg_ref, wout_ref,
                   tri_ref, gf_ref, fg_ref, w1c_hbm, w2c_hbm, out_ref,
                   z_scr, la_scr, dec_scr, etot_scr, og, state, s0s, hbuf, w1c_ref, w2c_ref, sem, *,
                   tile, n_tiles, tiles_per_seq, layer):
    s = pl.program_id(0)
    t_in_seq = lax.rem(s, tiles_per_seq)
    n_chunks = tile // CHUNK
    mlp_weights = _late_weight_copies(layer, w1c_hbm, w2c_hbm, w1c_ref, w2c_ref, sem)

    def input_projection(xv, n_pad=0):
        n_rows = xv.shape[0]
        u = _rms(xv, gm_ref[...]).astype(BF16)
        r = lax.dot_general(u, wrt_ref[...], _NT, preferred_element_type=F32)
        x = jnp.dot(r.astype(BF16), gw2_ref[...], preferred_element_type=F32) + gb_ref[...]
        la = (jnp.minimum(x, 0.0) - jnp.log1p(jnp.exp(-jnp.abs(x)))) * (1.0 / GLA_GATE_NORM)
        if n_pad:
            row = lax.broadcasted_iota(jnp.int32, (n_rows, GLA_DK), 0)
            la = jnp.where(row >= n_pad, la, 0.0)
        la_scr[0:n_rows, :] = la
        z_scr[0:n_rows, :] = lax.dot_general(u, wqkvgt_ref[...], _NT, preferred_element_type=F32)
        _gla_decay(la_scr, dec_scr, etot_scr, tri_ref, n_rows // CHUNK)

    def meta_state():
        state[...] = jnp.zeros(state.shape, F32)
        pad = jnp.zeros((CHUNK - N_META, D_MODEL), F32)
        input_projection(jnp.concatenate([pad, hm_ref[...]], axis=0), n_pad=CHUNK - N_META)
        _gla_update(0, 0, z_scr, dec_scr, etot_scr, state)
        s0s[...] = state[...]

    def output_projection():
        return hbuf[...] + jnp.dot(og[...], wout_ref[...], preferred_element_type=F32)

    def mlp_and_gla(h1, with_gla=True):
        uu = _rms(h1, gf_ref[...]).astype(BF16)
        acc = h1
        gla_chunks = n_chunks if with_gla else 0
        for c in range(max(N_FF, gla_chunks)):
            if c < N_FF:
                a = jnp.maximum(jnp.dot(uu, w1c_ref[c], preferred_element_type=F32), 0.0)
            if c < gla_chunks:
                _gla_update(c, c * CHUNK, z_scr, dec_scr, etot_scr, state)
            if c < N_FF:
                acc = acc + jnp.dot((a * a).astype(BF16), w2c_ref[c], preferred_element_type=F32)
            if c < gla_chunks:
                _gla_output(c * CHUNK, z_scr, og, state, hg_ref)
        out_ref[...] = _rms(acc, fg_ref[...])

    @pl.when(s == 0)
    def _():
        for cp in mlp_weights:
            cp.start()
        meta_state()
        input_projection(x_ref[...])
        for c in range(n_chunks):
            _gla_update(c, c * CHUNK, z_scr, dec_scr, etot_scr, state)
            _gla_output(c * CHUNK, z_scr, og, state, hg_ref)
        hbuf[...] = x_ref[...]

    @pl.when(s == 1)
    def _():
        for cp in mlp_weights:
            cp.wait()

    @pl.when(jnp.logical_and(s > 0, t_in_seq == 0))
    def _():
        state[...] = s0s[...]

    @pl.when(jnp.logical_and(s > 0, s < n_tiles))
    def _():
        h1 = output_projection()
        input_projection(x_ref[...])
        mlp_and_gla(h1)
        hbuf[...] = x_ref[...]

    @pl.when(s == n_tiles)
    def _():
        mlp_and_gla(output_projection(), with_gla=False)


def _layer1(x2d, seq, hm, gm, wqkvg_t, wr_t, gw2, gb, hg, wout, tri, gf, w1c, w2c, fg, *, tile):
    n, d = x2d.shape
    assert seq % tile == 0 and n % seq == 0 and tile % CHUNK == 0 and n > tile
    n_tiles = n // tile
    kern = functools.partial(_layer1_kernel, tile=tile, n_tiles=n_tiles, tiles_per_seq=seq // tile,
                             layer=w1c.layer)
    consts, const_specs = _split_consts((hm, gm, wqkvg_t, wr_t, gw2, gb, hg, wout, tri, gf, fg))
    late, late_specs, late_scratch = _late_operands(w1c, w2c)
    state_shape = (GLA_HEADS, GLA_HK, GLA_HV)
    in_map, out_map = _tile_maps(n_tiles)
    return pl.pallas_call(
        kern,
        grid=(n_tiles + 1,),
        in_specs=[pl.BlockSpec((tile, d), in_map)] + const_specs + late_specs,
        out_specs=pl.BlockSpec((tile, d), out_map),
        out_shape=jax.ShapeDtypeStruct(x2d.shape, F32),
        scratch_shapes=[pltpu.VMEM((tile, N_QKVG), F32),
                        pltpu.VMEM((tile, GLA_DK), F32),
                        pltpu.VMEM((tile, GLA_DK), F32),
                        pltpu.VMEM((tile // CHUNK, 8, GLA_DK), F32),
                        pltpu.VMEM((tile, GLA_DV), BF16),
                        pltpu.VMEM(state_shape, F32),
                        pltpu.VMEM(state_shape, F32),
                        pltpu.VMEM((tile, d), F32)] + late_scratch,
        compiler_params=pltpu.CompilerParams(dimension_semantics=("arbitrary",),
                                             vmem_limit_bytes=VMEM_LIMIT),
        name="layer1",
    )(x2d, *consts, *late)


def kernel(x, meta_tokens, mix_norm_g, ffn_norm_g, ffn_w1, ffn_w2, cp_w_in, cp_conv_w, cp_conv_b,
           cp_ln_g, cp_ln_b, cp_pool_w, cp_pool_scale, cp_w_out, gla_w_in, gla_gate_w2, gla_gate_b,
           gla_head_g, gla_w_out, final_norm_g):
    bsz, seq, d = x.shape
    row = lambda v: v.reshape(1, -1).astype(F32)

    depth = ffn_w1.shape[0]
    w1c = _to_bf16_col_chunks(ffn_w1)
    w2c = _to_bf16(ffn_w2, (2 * FF_CHUNK, d)).reshape(depth, N_FF, FF_CHUNK, d)
    cp_win = _Layer(_to_bf16(cp_w_in, (d, FF_CHUNK)), 0)
    cp_wout = _Layer(_to_bf16(cp_w_out, (d, FF_CHUNK)), 0)
    pw_bd = jnp.zeros((POOL_DIM, POOL_DIM), F32)
    for gi in range(len(POOL_WINDOWS)):
        lo = gi * POOL_GROUP
        pw_bd = lax.dynamic_update_slice(pw_bd, cp_pool_w[0, gi], (lo, lo))
    pw_bd = pw_bd.astype(BF16)
    w_in_t = jnp.swapaxes(gla_w_in[0], 0, 1)
    wqkvg_t = w_in_t[:N_QKVG].astype(BF16)
    wr_t = jnp.pad(w_in_t[N_QKVG:], ((0, LANES - GLA_GATE_RANK), (0, 0))).astype(BF16)
    gw2 = jnp.pad(gla_gate_w2[0], ((0, LANES - GLA_GATE_RANK), (0, 0))).astype(BF16)
    gla_wout = _Layer(_to_bf16(gla_w_out, (d, FF_CHUNK)), 0)
    tri = jnp.tril(jnp.ones((CHUNK, CHUNK), BF16))
    fg = row(final_norm_g)

    l0_args = (row(mix_norm_g[0]), cp_win, cp_conv_w[0].astype(F32), row(cp_conv_b[0]),
               row(cp_ln_g[0]), row(cp_ln_b[0]), pw_bd, row(cp_pool_scale[0]), cp_wout)
    l0_mlp = (row(ffn_norm_g[0]), _Layer(w1c, 0), _Layer(w2c, 0))
    l1_args = (row(mix_norm_g[1]), wqkvg_t, wr_t, gw2, row(gla_gate_b[0]), row(gla_head_g[0]), gla_wout)
    l1_mlp = (row(ffn_norm_g[1]), _Layer(w1c, 1), _Layer(w2c, 1), fg)

    tile = 512
    h, hm = _layer0(x.astype(F32).reshape(bsz * seq, d), seq, meta_tokens.astype(F32), *l0_args,
                    *l0_mlp, tile=tile)
    out = _layer1(h, seq, hm, *l1_args, tri, *l1_mlp, tile=tile)
    return out.reshape(bsz, seq, d).astype(x.dtype)
```
